```python
import math
import jax, jax.numpy as jnp
from jax import lax
import numpy as np

D_MODEL = 1024
BATCH = 1
SEQ = 16384
DEPTH = 2
DEC_BATCH = 32
DEC_SEQ = 8
PAST_LEN = 16384
PAGE_SIZE = 128

BRANCH_WIDTH = D_MODEL // 2
HEAD_DIM = 64
N_HEADS = BRANCH_WIDTH // HEAD_DIM
N_IDX_HEADS = 8
IDX_DIM = 64
TOPK_MAX = 256
ROPE_THETA = 500000.0
ROPE_FRACTION = 4
ML_QK_DIM = 64
ML_V_DIM = 128
ML_HEADS = BRANCH_WIDTH // ML_V_DIM
ML_CHUNK = 64
N_BRANCH = 2
FF_DIM = -(-8 * D_MODEL // (3 * 256)) * 256
Q_BLOCK = 128
EPS = 1e-6
IN_SIZES = (BRANCH_WIDTH, BRANCH_WIDTH, BRANCH_WIDTH,
            N_IDX_HEADS * IDX_DIM, IDX_DIM, N_IDX_HEADS,
            ML_HEADS * ML_QK_DIM, ML_HEADS * ML_QK_DIM,
            BRANCH_WIDTH, ML_HEADS, ML_HEADS, BRANCH_WIDTH,
            N_BRANCH * D_MODEL)
IN_COLS = sum(IN_SIZES)

kernel_name = 'dsa_mlstm_hybrid_step'


def rmsnorm(x, g):
    xf = x.astype(jnp.float32)
    y = xf * lax.rsqrt(jnp.mean(xf * xf, axis=-1, keepdims=True) + EPS) * g.astype(jnp.float32)
    return y.astype(x.dtype)


def rope_partial(x, pos):
    d = x.shape[-1]
    r = d // ROPE_FRACTION
    half = r // 2
    freqs = ROPE_THETA ** (-jnp.arange(half, dtype=jnp.float32) / half)
    ang = pos.astype(jnp.float32)[:, None] * freqs[None, :]
    cos = jnp.cos(ang)[None, :, None, :]
    sin = jnp.sin(ang)[None, :, None, :]
    xf = x.astype(jnp.float32)
    x1, x2, rest = xf[..., :half], xf[..., half:r], xf[..., r:]
    return jnp.concatenate([x1 * cos - x2 * sin, x2 * cos + x1 * sin, rest], axis=-1).astype(x.dtype)


def mixer_inputs(xn, pos, w_in_l, q_norm_l, k_norm_l, gate_bias_l):
    B, L, _ = xn.shape
    split_at = np.cumsum(IN_SIZES)[:-1].tolist()
    aq, ak, av, iq, ik, iw, mq, mk, mv, mi, mf, mo, gates = jnp.split(xn @ w_in_l, split_at, axis=-1)
    aq = rope_partial(rmsnorm(aq.reshape(B, L, N_HEADS, HEAD_DIM), q_norm_l), pos)
    ak = rope_partial(rmsnorm(ak.reshape(B, L, N_HEADS, HEAD_DIM), k_norm_l), pos)
    av = av.reshape(B, L, N_HEADS, HEAD_DIM)
    iq = rope_partial(iq.reshape(B, L, N_IDX_HEADS, IDX_DIM), pos)
    ik = rope_partial(ik[:, :, None, :], pos)[:, :, 0, :]
    mq = mq.reshape(B, L, ML_HEADS, ML_QK_DIM)
    mk = mk.reshape(B, L, ML_HEADS, ML_QK_DIM)
    mv = mv.reshape(B, L, ML_HEADS, ML_V_DIM)
    mi = mi + gate_bias_l[:ML_HEADS]
    mf = mf + gate_bias_l[ML_HEADS:]
    return aq, ak, av, iq, ik, iw, mq, mk, mv, mi, mf, mo, gates


def indexer_scores(iq, iw, ik, q_pos, k_pos):
    s = jnp.einsum('bqhd,bsd->bqhs', iq.astype(jnp.float32), ik.astype(jnp.float32)) * IDX_DIM ** -0.5
    score = jnp.einsum('bqhs,bqh->bqs', jax.nn.relu(s), iw.astype(jnp.float32) * N_IDX_HEADS ** -0.5)
    return jnp.where(k_pos[None, None, :] <= q_pos[None, :, None], score, -jnp.inf)


def sparse_attend(q, k_sel, v_sel, valid):
    B, Q = q.shape[:2]
    logits = jnp.einsum('bqhd,bqkhd->bqhk', q.astype(jnp.float32), k_sel.astype(jnp.float32)) * HEAD_DIM ** -0.5
    logits = jnp.where(valid[:, :, None, :], logits, -jnp.inf)
    p = jax.nn.softmax(logits, axis=-1)
    out = jnp.einsum('bqhk,bqkhd->bqhd', p, v_sel.astype(jnp.float32))
    return out.reshape(B, Q, N_HEADS * HEAD_DIM).astype(q.dtype)


def gather_rows(t, idx):
    return jax.vmap(lambda tb, ib: tb[ib])(t, idx)


def dsa_prompt(aq, ak, av, iq, ik, iw):
    B, L = aq.shape[:2]
    topk = min(TOPK_MAX, L // 4)
    nblk = L // Q_BLOCK
    k_pos = jnp.arange(L)

    def block(args):
        b_id, q_b, iq_b, iw_b = args
        q_pos = b_id * Q_BLOCK + jnp.arange(Q_BLOCK)
        _, idx = lax.top_k(indexer_scores(iq_b, iw_b, ik, q_pos, k_pos), topk)
        valid = idx <= q_pos[None, :, None]
        return sparse_attend(q_b, gather_rows(ak, idx), gather_rows(av, idx), valid)

    def to_blocks(t):
        return jnp.moveaxis(t.reshape(B, nblk, Q_BLOCK, *t.shape[2:]), 1, 0)

    out = lax.map(block, (jnp.arange(nblk), to_blocks(aq), to_blocks(iq), to_blocks(iw)))
    return jnp.moveaxis(out, 0, 1).reshape(B, L, BRANCH_WIDTH)


def dsa_sample(aq, ak, av, iq, ik, iw, cache_k_l, cache_v_l, cache_ik_l, page_table):
    Bd, Q = aq.shape[:2]
    past = page_table.shape[1] * PAGE_SIZE
    L = past + Q
    topk = min(TOPK_MAX, L // 4)
    ik_past = cache_ik_l[page_table].reshape(Bd, past, IDX_DIM)
    ik_all = jnp.concatenate([ik_past.astype(ik.dtype), ik], axis=1)
    q_pos = past + jnp.arange(Q)
    _, idx = lax.top_k(indexer_scores(iq, iw, ik_all, q_pos, jnp.arange(L)), topk)
    valid = idx <= q_pos[None, :, None]
    is_past = (idx < past)[..., None, None]
    pidx = jnp.minimum(idx, past - 1)
    phys = page_table[jnp.arange(Bd)[:, None, None], pidx // PAGE_SIZE]
    off = pidx % PAGE_SIZE
    nidx = jnp.clip(idx - past, 0, Q - 1)
    k_sel = jnp.where(is_past, cache_k_l[phys, off].astype(ak.dtype), gather_rows(ak, nidx))
    v_sel = jnp.where(is_past, cache_v_l[phys, off].astype(av.dtype), gather_rows(av, nidx))
    return sparse_attend(aq, k_sel, v_sel, valid)


def mlstm(q, k, v, ig, fg, C0, n0, m0):
    B, L, H, dk = q.shape
    dv = v.shape[-1]
    lc = math.gcd(L, ML_CHUNK)
    nc = L // lc
    f32 = jnp.float32

    def chunks(t):
        t = t.astype(f32).reshape(B, nc, lc, H, *t.shape[3:])
        return jnp.swapaxes(jnp.moveaxis(t, 1, 0), 2, 3)

    qc = chunks(q) * dk ** -0.5
    kc, vc, igc = chunks(k), chunks(v), chunks(ig)
    lfc = jax.nn.log_sigmoid(chunks(fg))
    tri = jnp.tril(jnp.ones((lc, lc), dtype=bool))

    def step(carry, xs):
        C, n, m = carry
        qt, kt, vt, it, lf = xs
        b = jnp.cumsum(lf, axis=-1)
        dmat = jnp.where(tri, b[..., :, None] - b[..., None, :] + it[..., None, :], -jnp.inf)
        inter = b + m[..., None]
        mt = jnp.maximum(inter, dmat.max(-1))
        s = jnp.einsum('bhtd,bhsd->bhts', qt, kt) * jnp.exp(dmat - mt[..., None])
        w = jnp.exp(inter - mt)
        num = jnp.einsum('bhts,bhsv->bhtv', s, vt) + w[..., None] * jnp.einsum('bhtd,bhdv->bhtv', qt, C)
        den = s.sum(-1) + w * jnp.einsum('bhtd,bhd->bht', qt, n)
        h = num / jnp.maximum(jnp.abs(den), jnp.exp(-mt))[..., None]
        bl = b[..., -1]
        g = bl[..., None] - b + it
        m_new = jnp.maximum(bl + m, g.max(-1))
        decay = jnp.exp(bl + m - m_new)
        wk = jnp.exp(g - m_new[..., None])
        C_new = decay[..., None, None] * C + jnp.einsum('bhs,bhsd,bhsv->bhdv', wk, kt, vt)
        n_new = decay[..., None] * n + jnp.einsum('bhs,bhsd->bhd', wk, kt)
        return (C_new, n_new, m_new), h

    (C, n, m), hc = lax.scan(step, (C0.astype(f32), n0.astype(f32), m0.astype(f32)), (qc, kc, vc, igc, lfc))
    h = jnp.moveaxis(jnp.swapaxes(hc, 2, 3), 0, 1).reshape(B, L, H * dv)
    return h.astype(q.dtype), C, n, m


def merge_branches(a_out, m_h, mo, gates, w_branch_l, w_out_l):
    B, L = a_out.shape[:2]
    g = jax.nn.sigmoid(gates.reshape(B, L, N_BRANCH, D_MODEL))
    m_out = jax.nn.sigmoid(mo) * m_h
    merged = g[:, :, 0] * (a_out @ w_branch_l[0]) + g[:, :, 1] * (m_out @ w_branch_l[1])
    return merged @ w_out_l


def swiglu(xn, w_gu, w_d):
    gate, up = jnp.split(xn @ w_gu, 2, axis=-1)
    return (jax.nn.silu(gate) * up) @ w_d


def setup_inputs(seed: int = 0) -> dict:
    key = jax.random.key(seed)
    ks = jax.random.split(key, 24)
    f32 = jnp.float32
    n_pages = PAST_LEN // PAGE_SIZE
    n_used = DEC_BATCH * n_pages
    n_pool = n_used + max(1, n_used // 4)

    def nrm(k, shape, s=1.0):
        return s * jax.random.normal(k, shape, f32)

    x_prompt = nrm(ks[0], (BATCH, SEQ, D_MODEL))
    x_sample = nrm(ks[1], (DEC_BATCH, DEC_SEQ, D_MODEL))
    cache_k = nrm(ks[2], (DEPTH, n_pool, PAGE_SIZE, N_HEADS, HEAD_DIM))
    cache_v = nrm(ks[3], (DEPTH, n_pool, PAGE_SIZE, N_HEADS, HEAD_DIM))
    cache_idx_k = nrm(ks[4], (DEPTH, n_pool, PAGE_SIZE, IDX_DIM))
    state_C = nrm(ks[5], (DEPTH, DEC_BATCH, ML_HEADS, ML_QK_DIM, ML_V_DIM), 0.5)
    state_n = nrm(ks[6], (DEPTH, DEC_BATCH, ML_HEADS, ML_QK_DIM), 0.5)
    state_m = nrm(ks[7], (DEPTH, DEC_BATCH, ML_HEADS), 0.5)
    page_table = jax.random.permutation(ks[8], n_pool)[:n_used].reshape(DEC_BATCH, n_pages).astype(jnp.int32)
    norm_mix = 1.0 + nrm(ks[9], (DEPTH, D_MODEL), 0.02)
    w_in = nrm(ks[10], (DEPTH, D_MODEL, IN_COLS), D_MODEL ** -0.5)
    q_norm = 1.0 + nrm(ks[11], (DEPTH, HEAD_DIM), 0.02)
    k_norm = 1.0 + nrm(ks[12], (DEPTH, HEAD_DIM), 0.02)
    gate_bias = jnp.concatenate([-1.0 + nrm(ks[13], (DEPTH, ML_HEADS), 0.1),
                                 3.0 + nrm(ks[14], (DEPTH, ML_HEADS), 0.5)], axis=-1)
    w_branch = nrm(ks[15], (DEPTH, N_BRANCH, BRANCH_WIDTH, D_MODEL), BRANCH_WIDTH ** -0.5)
    w_out = nrm(ks[16], (DEPTH, D_MODEL, D_MODEL), D_MODEL ** -0.5)
    norm_ffn = 1.0 + nrm(ks[17], (DEPTH, D_MODEL), 0.02)
    w_gate_up = nrm(ks[18], (DEPTH, D_MODEL, 2 * FF_DIM), D_MODEL ** -0.5)
    w_down = nrm(ks[19], (DEPTH, FF_DIM, D_MODEL), FF_DIM ** -0.5)
    return {'x_prompt': x_prompt, 'x_sample': x_sample, 'cache_k': cache_k, 'cache_v': cache_v,
            'cache_idx_k': cache_idx_k, 'state_C': state_C, 'state_n': state_n, 'state_m': state_m,
            'page_table': page_table, 'norm_mix': norm_mix, 'w_in': w_in, 'q_norm': q_norm,
            'k_norm': k_norm, 'gate_bias': gate_bias, 'w_branch': w_branch, 'w_out': w_out,
            'norm_ffn': norm_ffn, 'w_gate_up': w_gate_up, 'w_down': w_down}


def reference(x_prompt, x_sample, cache_k, cache_v, cache_idx_k, state_C, state_n, state_m, page_table,
              norm_mix, w_in, q_norm, k_norm, gate_bias, w_branch, w_out, norm_ffn, w_gate_up, w_down):
    xp, xs = x_prompt, x_sample
    Bp, Lp = xp.shape[:2]
    Bd, Ld = xs.shape[:2]
    past = page_table.shape[1] * PAGE_SIZE
    pos_p = jnp.arange(Lp)
    pos_s = past + jnp.arange(Ld)
    kp, vp, ikp, Cp, np_, mp = [], [], [], [], [], []
    ksl, vsl, iks, Cs, ns, ms = [], [], [], [], [], []
    for l in range(DEPTH):
        aq, ak, av, iq, ik, iw, mq, mk, mv, mi, mf, mo, gates = mixer_inputs(
            rmsnorm(xp, norm_mix[l]), pos_p, w_in[l], q_norm[l], k_norm[l], gate_bias[l])
        a_out = dsa_prompt(aq, ak, av, iq, ik, iw)
        m_h, C, n, m = mlstm(mq, mk, mv, mi, mf,
                             jnp.zeros((Bp, ML_HEADS, ML_QK_DIM, ML_V_DIM), jnp.float32),
                             jnp.zeros((Bp, ML_HEADS, ML_QK_DIM), jnp.float32),
                             jnp.zeros((Bp, ML_HEADS), jnp.float32))
        xp = xp + merge_branches(a_out, m_h, mo, gates, w_branch[l], w_out[l]).astype(xp.dtype)
        xp = xp + swiglu(rmsnorm(xp, norm_ffn[l]), w_gate_up[l], w_down[l]).astype(xp.dtype)
        kp.append(ak); vp.append(av); ikp.append(ik); Cp.append(C); np_.append(n); mp.append(m)
        aq, ak, av, iq, ik, iw, mq, mk, mv, mi, mf, mo, gates = mixer_inputs(
            rmsnorm(xs, norm_mix[l]), pos_s, w_in[l], q_norm[l], k_norm[l], gate_bias[l])
        a_out = dsa_sample(aq, ak, av, iq, ik, iw, cache_k[l], cache_v[l], cache_idx_k[l], page_table)
        m_h, C, n, m = mlstm(mq, mk, mv, mi, mf, state_C[l], state_n[l], state_m[l])
        xs = xs + merge_branches(a_out, m_h, mo, gates, w_branch[l], w_out[l]).astype(xs.dtype)
        xs = xs + swiglu(rmsnorm(xs, norm_ffn[l]), w_gate_up[l], w_down[l]).astype(xs.dtype)
        ksl.append(ak); vsl.append(av); iks.append(ik); Cs.append(C); ns.append(n); ms.append(m)
    return (xp, xs,
            jnp.stack(kp), jnp.stack(vp), jnp.stack(ikp), jnp.stack(Cp), jnp.stack(np_), jnp.stack(mp),
            jnp.stack(ksl), jnp.stack(vsl), jnp.stack(iks), jnp.stack(Cs), jnp.stack(ns), jnp.stack(ms))
```

```python
import functools

import numpy as np
import jax
import jax.numpy as jnp
from jax import lax
from jax.experimental import pallas as pl
from jax.experimental.pallas import tpu as pltpu

D_MODEL = 1024
BRANCH_WIDTH = D_MODEL // 2
HEAD_DIM = 64
N_HEADS = BRANCH_WIDTH // HEAD_DIM
N_IDX_HEADS = 8
IDX_DIM = 64
TOPK_MAX = 256
ROPE_THETA = 500000.0
ROPE_FRACTION = 4
ML_QK_DIM = 64
ML_V_DIM = 128
ML_HEADS = BRANCH_WIDTH // ML_V_DIM
ML_CHUNK = 64
FF_DIM = -(-8 * D_MODEL // (3 * 256)) * 256
PAGE_SIZE = 128
EPS = 1e-6

LANES = 128
VMEM_LIMIT = 56 * 1024 * 1024
INT_MIN = -2 ** 31
NEG_BIG = -1e30

_C_AQ, _C_IK, _C_IW, _C_MQ, _C_MI, _C_MO = 0, 2048, 2112, 2120, 3144, 3152
_IN_COLS = 5712
_M_IW, _M_MI, _M_MF = 64, 72, 76

F32 = jnp.float32
BF16 = jnp.bfloat16
I32 = jnp.int32


def _cparams(sem):
    return pltpu.CompilerParams(dimension_semantics=sem, vmem_limit_bytes=VMEM_LIMIT)


def _pick_tile(n, prefs):
    for t in prefs:
        if n % t == 0:
            return t
    return n


def _const_spec(shape):
    nd = len(shape)
    return pl.BlockSpec(shape, lambda *a: (0,) * nd)


def _inproj_kernel(x_ref, g_ref, wmain_ref, wmisc_ref, qn_ref, kn_ref, bias_ref, cos_ref, sin_ref, bd_ref,
                   aq_ref, akf_ref, akb_ref, avf_ref, avb_ref, iq_ref, misc_ref, ikT_ref,
                   mq_ref, mk_ref, mv_ref, mo_ref, gates_ref):
    x = x_ref[...]
    tm = x.shape[0]
    xn = (x * lax.rsqrt(jnp.mean(x * x, axis=-1, keepdims=True) + EPS) * g_ref[...]).astype(BF16)
    cosv = cos_ref[...]
    sinv = sin_ref[...]
    lane = lax.broadcasted_iota(I32, (tm, LANES), 1)
    first = (lane % HEAD_DIM) < (HEAD_DIM // ROPE_FRACTION // 2)
    shift = HEAD_DIM // ROPE_FRACTION // 2

    def rope(y, c, s):
        partner = jnp.where(first, pltpu.roll(y, LANES - shift, 1), pltpu.roll(y, shift, 1))
        return y * c + partner * s

    def headnorm(y, gain):
        ms = jnp.dot(y * y, bd_ref[...], preferred_element_type=F32, precision=lax.Precision.HIGHEST)
        return y * lax.rsqrt(ms + EPS) * gain

    def proj(c0, c1):
        return jnp.dot(xn, wmain_ref[:, c0:c1], preferred_element_type=F32)

    yq = proj(0, 512)
    yk = proj(512, 1024)
    yi = proj(1536, 2048)
    for g in range(4):
        sl = slice(g * LANES, (g + 1) * LANES)
        q = rope(headnorm(yq[:, sl], qn_ref[...]), cosv, sinv)
        aq_ref[:, sl] = (q * (HEAD_DIM ** -0.5)).astype(BF16)
        k = rope(headnorm(yk[:, sl], kn_ref[...]), cosv, sinv)
        akf_ref[:, sl] = k
        akb_ref[:, sl] = k.astype(BF16)
        iq_ref[:, sl] = (rope(yi[:, sl], cosv, sinv) * (IDX_DIM ** -0.5)).astype(BF16)
    yv = proj(1024, 1536)
    avf_ref[...] = yv
    avb_ref[...] = yv.astype(BF16)
    mq_ref[...] = proj(2048, 2304)
    mk_ref[...] = proj(2304, 2560)
    mv_ref[...] = proj(2560, 3072)
    mo_ref[...] = proj(3072, 3584)
    gates_ref[...] = proj(3584, 5632)
    ym = jnp.dot(xn, wmisc_ref[...], preferred_element_type=F32) + bias_ref[...]
    is_ik = lane < IDX_DIM
    ym = rope(ym, jnp.where(is_ik, cosv, 1.0), jnp.where(is_ik, sinv, 0.0))
    misc_ref[...] = ym
    ikT_ref[...] = jnp.transpose(ym)[:IDX_DIM, :].astype(BF16)


def _rope_tables(pos):
    half = HEAD_DIM // ROPE_FRACTION // 2
    freqs = ROPE_THETA ** (-jnp.arange(half, dtype=F32) / half)
    ang = pos.astype(F32)[:, None] * freqs[None, :]
    cos, sin = jnp.cos(ang), jnp.sin(ang)
    n = pos.shape[0]
    ones = jnp.ones((n, HEAD_DIM - 2 * half), F32)
    c = jnp.concatenate([cos, cos, ones], axis=1)
    s = jnp.concatenate([-sin, sin, 0.0 * ones], axis=1)
    return jnp.tile(c, (1, LANES // HEAD_DIM)), jnp.tile(s, (1, LANES // HEAD_DIM))


def _inproj(x, pos_tables, g, wmain, wmisc, qn, kn, bias, bd):
    R = x.shape[0]
    tm = _pick_tile(R, (256, 128))
    cos_t, sin_t = pos_tables
    row = lambda w: pl.BlockSpec((tm, w), lambda i: (i, 0))
    out_shapes = [
        jax.ShapeDtypeStruct((R, 512), BF16),
        jax.ShapeDtypeStruct((R, 512), F32),
        jax.ShapeDtypeStruct((R, 512), BF16),
        jax.ShapeDtypeStruct((R, 512), F32),
        jax.ShapeDtypeStruct((R, 512), BF16),
        jax.ShapeDtypeStruct((R, 512), BF16),
        jax.ShapeDtypeStruct((R, LANES), F32),
        jax.ShapeDtypeStruct((IDX_DIM, R), BF16),
        jax.ShapeDtypeStruct((R, 256), F32),
        jax.ShapeDtypeStruct((R, 256), F32),
        jax.ShapeDtypeStruct((R, 512), F32),
        jax.ShapeDtypeStruct((R, 512), F32),
        jax.ShapeDtypeStruct((R, 2048), F32),
    ]
    out_specs = [row(512), row(512), row(512), row(512), row(512), row(512), row(LANES),
                 pl.BlockSpec((IDX_DIM, tm), lambda i: (0, i)),
                 row(256), row(256), row(512), row(512), row(2048)]
    in_specs = [row(D_MODEL), _const_spec((1, D_MODEL)), _const_spec(wmain.shape), _const_spec(wmisc.shape),
                _const_spec((1, LANES)), _const_spec((1, LANES)), _const_spec((1, LANES)),
                row(LANES), row(LANES), _const_spec((LANES, LANES))]
    return pl.pallas_call(
        _inproj_kernel, grid=(R // tm,), in_specs=in_specs, out_specs=out_specs, out_shape=out_shapes,
        compiler_params=_cparams(("parallel",)), name="inproj",
    )(x, g, wmain, wmisc, qn, kn, bias, cos_t, sin_t, bd)


def _f32_key(x):
    bits = pltpu.bitcast(x, I32)
    return bits ^ (lax.shift_right_arithmetic(bits, 31) & 0x7FFFFFFF)


def _count_rows(keys_ref, r0, rows, n128, pred):
    def body(c, acc):
        k0 = pl.multiple_of(c * LANES, LANES)
        return acc + pred(keys_ref[r0:r0 + rows, pl.ds(k0, LANES)], k0).astype(I32)
    acc = lax.fori_loop(0, n128, body, jnp.zeros((rows, LANES), I32))
    return jnp.sum(acc, axis=1, keepdims=True)


def _topk_select(keys_ref, r0, rows, n128, kprime, n_idx_bits):
    def bit_pass(bi, t_u):
        bit = lax.shift_left(jnp.int32(1), 31 - bi)
        cand = jnp.broadcast_to((t_u | bit) ^ INT_MIN, (rows, LANES))
        cnt = _count_rows(keys_ref, r0, rows, n128, lambda kk, k0: kk >= cand)
        return jnp.where(cnt >= kprime, t_u | bit, t_u)

    t_u = lax.fori_loop(0, 32, bit_pass, jnp.zeros((rows, 1), I32))
    thr = t_u ^ INT_MIN
    thr_b = jnp.broadcast_to(thr, (rows, LANES))
    n_gt = _count_rows(keys_ref, r0, rows, n128, lambda kk, k0: kk > thr_b)
    need = kprime - n_gt
    need_b = jnp.broadcast_to(need, (rows, 1))
    lane = lax.broadcasted_iota(I32, (rows, LANES), 1)

    def idx_pass(bi, j):
        bit = lax.shift_left(jnp.int32(1), n_idx_bits - 1 - bi)
        cand = jnp.broadcast_to(j | bit, (rows, LANES))
        cnt = _count_rows(keys_ref, r0, rows, n128, lambda kk, k0: (kk == thr_b) & ((lane + k0) < cand))
        return jnp.where(cnt < need_b, j | bit, j)

    j_sel = lax.fori_loop(0, n_idx_bits, idx_pass, jnp.zeros((rows, 1), I32))
    return thr, j_sel


def _dsa_prompt_kernel(qi_ref, kj_ref, last_ref,
                       aq_ref, iq_ref, misc_ref, ikT_ref, k_ref, v_ref, out_ref,
                       keys_sc, thr_sc, jsel_sc, m_sc, l_sc, acc_sc, *, tq, kc, topk, seq_len):
    s_id = pl.program_id(0)
    qi = qi_ref[s_id]
    kj = kj_ref[s_id]
    n_idx_bits = int(seq_len).bit_length()
    q_pos = qi * tq + lax.broadcasted_iota(I32, (tq, 1), 0)

    @pl.when(kj == 0)
    def _select():
        n_kc = (qi * tq + tq + kc - 1) // kc
        w_all = misc_ref[:, _M_IW:_M_IW + N_IDX_HEADS] * (N_IDX_HEADS ** -0.5)

        def score_body(c, carry):
            k0 = pl.multiple_of(c * kc, kc)
            ikc = ikT_ref[:, pl.ds(k0, kc)]
            acc = jnp.zeros((tq, kc), F32)
            for h in range(N_IDX_HEADS):
                s = jnp.dot(iq_ref[:, h * IDX_DIM:(h + 1) * IDX_DIM], ikc, preferred_element_type=F32)
                acc = acc + jnp.maximum(s, 0.0) * w_all[:, h:h + 1]
            k_pos = k0 + lax.broadcasted_iota(I32, (1, kc), 1)
            keys_sc[:, pl.ds(k0, kc)] = jnp.where(k_pos <= q_pos, _f32_key(acc), INT_MIN)
            return carry

        lax.fori_loop(0, n_kc, score_body, 0)
        n128 = n_kc * (kc // LANES)
        kprime = jnp.minimum(topk, q_pos + 1)
        rb = min(tq, LANES)
        for r0 in range(0, tq, rb):
            thr, j_sel = _topk_select(keys_sc, r0, rb, n128, kprime[r0:r0 + rb], n_idx_bits)
            thr_sc[r0:r0 + rb, :] = jnp.broadcast_to(thr, (rb, LANES))
            jsel_sc[r0:r0 + rb, :] = jnp.broadcast_to(j_sel, (rb, LANES))
        m_sc[...] = jnp.full(m_sc.shape, NEG_BIG, F32)
        l_sc[...] = jnp.zeros(l_sc.shape, F32)
        acc_sc[...] = jnp.zeros(acc_sc.shape, F32)

    k0 = pl.multiple_of(kj * kc, kc)
    kk = keys_sc[:, pl.ds(k0, kc)]
    thr = thr_sc[:, 0:1]
    k_pos = k0 + lax.broadcasted_iota(I32, (1, kc), 1)
    sel = (kk > thr) | ((kk == thr) & (k_pos <= jsel_sc[:, 0:1]))
    for h in range(N_HEADS):
        hs = slice(h * HEAD_DIM, (h + 1) * HEAD_DIM)
        logits = lax.dot_general(aq_ref[:, hs], k_ref[:, hs], (((1,), (1,)), ((), ())), preferred_element_type=F32)
        lm = jnp.where(sel, logits, NEG_BIG)
        m_old = m_sc[h][:, 0:1]
        m_new = jnp.maximum(m_old, jnp.max(lm, axis=1, keepdims=True))
        p = jnp.exp(lm - m_new)
        alpha = jnp.exp(m_old - m_new)
        l_sc[h] = jnp.broadcast_to(alpha * l_sc[h][:, 0:1] + jnp.sum(p, axis=1, keepdims=True), (tq, LANES))
        acc_sc[h] = alpha * acc_sc[h] + jnp.dot(p.astype(BF16), v_ref[:, hs], preferred_element_type=F32)
        m_sc[h] = jnp.broadcast_to(m_new, (tq, LANES))

    @pl.when(last_ref[s_id] == 1)
    def _finish():
        for h in range(N_HEADS):
            out_ref[:, h * HEAD_DIM:(h + 1) * HEAD_DIM] = (acc_sc[h] / l_sc[h][:, 0:1]).astype(out_ref.dtype)


def _dsa_prompt(aq, iq, misc, ikT, akb, avb):
    L = aq.shape[0]
    tq = _pick_tile(L, (256, 128))
    kc = _pick_tile(L, (512, 256, 128))
    topk = min(TOPK_MAX, L // 4)
    qi, kj, last = [], [], []
    for i in range(L // tq):
        n_kc = ((i + 1) * tq + kc - 1) // kc
        for j in range(n_kc):
            qi.append(i), kj.append(j), last.append(int(j == n_kc - 1))
    qi, kj, last = (jnp.asarray(np.asarray(a, np.int32)) for a in (qi, kj, last))
    qrow = lambda w: pl.BlockSpec((tq, w), lambda s, qi, kj, last: (qi[s], 0))
    krow = pl.BlockSpec((kc, 512), lambda s, qi, kj, last: (kj[s], 0))
    grid_spec = pltpu.PrefetchScalarGridSpec(
        num_scalar_prefetch=3, grid=(int(qi.shape[0]),),
        in_specs=[qrow(512), qrow(512), qrow(LANES), pl.BlockSpec((IDX_DIM, L), lambda s, qi, kj, last: (0, 0)),
                  krow, krow],
        out_specs=qrow(512),
        scratch_shapes=[pltpu.VMEM((tq, L), I32), pltpu.VMEM((tq, LANES), I32), pltpu.VMEM((tq, LANES), I32),
                        pltpu.VMEM((N_HEADS, tq, LANES), F32), pltpu.VMEM((N_HEADS, tq, LANES), F32),
                        pltpu.VMEM((N_HEADS, tq, HEAD_DIM), F32)])
    return pl.pallas_call(
        functools.partial(_dsa_prompt_kernel, tq=tq, kc=kc, topk=topk, seq_len=L),
        grid_spec=grid_spec, out_shape=jax.ShapeDtypeStruct((L, 512), BF16),
        compiler_params=_cparams(("arbitrary",)), name="dsa_prompt",
    )(qi, kj, last, aq, iq, misc, ikT, akb, avb)


def _dsa_sample_select_kernel(pt_ref, iqs_ref, wst_ref, iknew_ref, *rest, pages, n_pages, topk, past, nq):
    page_refs = rest[:pages]
    keys_ref, thr_ref, jsel_ref = rest[pages:pages + 3]
    keys_sc = rest[pages + 3]
    j = pl.program_id(1)
    iqs = iqs_ref[0]
    w = wst_ref[0][:, 0:1] * (N_IDX_HEADS ** -0.5)

    def scores(ik_t):
        s = jnp.dot(iqs, ik_t, preferred_element_type=F32)
        t = jnp.maximum(s, 0.0) * w
        acc = jnp.zeros((nq, LANES), F32)
        for h in range(N_IDX_HEADS):
            acc = acc + t[h * nq:(h + 1) * nq, :]
        return acc

    for p in range(pages):
        k0 = pl.multiple_of((j * pages + p) * PAGE_SIZE, PAGE_SIZE)
        keys_sc[:, pl.ds(k0, PAGE_SIZE)] = _f32_key(scores(page_refs[p][0].astype(BF16)))

    @pl.when(j == n_pages // pages - 1)
    def _finish():
        qrow = lax.broadcasted_iota(I32, (nq, LANES), 0)
        lane = lax.broadcasted_iota(I32, (nq, LANES), 1)
        keys_sc[:, past:past + LANES] = jnp.where(lane <= qrow, _f32_key(scores(iknew_ref[0])), INT_MIN)
        kprime = jnp.minimum(topk, past + 1 + lax.broadcasted_iota(I32, (nq, 1), 0))
        thr, j_sel = _topk_select(keys_sc, 0, nq, n_pages + 1, kprime, int(past + LANES).bit_length())
        keys_ref[0] = keys_sc[...]
        thr_ref[0] = jnp.broadcast_to(thr, (nq, LANES))
        jsel_ref[0] = jnp.broadcast_to(j_sel, (nq, LANES))


def _dsa_sample_select(page_flat, iqs, wst, iknew, cache_ik2, layer, n_pool, n_pages, nq):
    Bd = iqs.shape[0]
    pages = _pick_tile(n_pages, (16, 8, 4, 2, 1))
    past = n_pages * PAGE_SIZE
    topk = min(TOPK_MAX, (past + nq) // 4)
    base = layer * n_pool

    def page_spec(p):
        return pl.BlockSpec((1, IDX_DIM, PAGE_SIZE),
                            lambda b, j, pt: (base + pt[b * n_pages + j * pages + p], 0, 0))

    per_b = lambda shape: pl.BlockSpec((1,) + shape, lambda b, j, pt: (b, 0, 0))
    grid_spec = pltpu.PrefetchScalarGridSpec(
        num_scalar_prefetch=1, grid=(Bd, n_pages // pages),
        in_specs=[per_b((N_IDX_HEADS * nq, IDX_DIM)), per_b((N_IDX_HEADS * nq, LANES)), per_b((IDX_DIM, LANES))]
                 + [page_spec(p) for p in range(pages)],
        out_specs=[per_b((nq, past + LANES)), per_b((nq, LANES)), per_b((nq, LANES))],
        scratch_shapes=[pltpu.VMEM((nq, past + LANES), I32)])
    return pl.pallas_call(
        functools.partial(_dsa_sample_select_kernel, pages=pages, n_pages=n_pages, topk=topk, past=past, nq=nq),
        grid_spec=grid_spec,
        out_shape=[jax.ShapeDtypeStruct((Bd, nq, past + LANES), I32), jax.ShapeDtypeStruct((Bd, nq, LANES), I32),
                   jax.ShapeDtypeStruct((Bd, nq, LANES), I32)],
        compiler_params=_cparams(("arbitrary", "arbitrary")), name="dsa_sample_select",
    )(page_flat, iqs, wst, iknew, *([cache_ik2] * pages))


def _dsa_sample_attend_kernel(pt_ref, qbd_ref, keys_ref, thr_ref, jsel_ref, knew_ref, vnew_ref, *rest,
                              pages, n_steps, past, nq):
    k_refs = rest[:pages]
    v_refs = rest[pages:2 * pages]
    out_ref = rest[2 * pages]
    m_sc, l_sc, acc_sc = rest[2 * pages + 1:]
    j = pl.program_id(1)
    rows = N_HEADS * nq
    qbd = qbd_ref[0]
    thr = thr_ref[0][:, 0:1]
    j_sel = jsel_ref[0][:, 0:1]

    @pl.when(j == 0)
    def _init():
        m_sc[...] = jnp.full(m_sc.shape, NEG_BIG, F32)
        l_sc[...] = jnp.zeros(l_sc.shape, F32)
        acc_sc[...] = jnp.zeros(acc_sc.shape, F32)

    def attend(k_list, v_list, k0):
        width = LANES * len(k_list)
        kk = keys_ref[0, :, pl.ds(k0, width)]
        k_pos = k0 + lax.broadcasted_iota(I32, (1, width), 1)
        sel = (kk > thr) | ((kk == thr) & (k_pos <= j_sel))
        sel = jnp.concatenate([sel.astype(I32)] * N_HEADS, axis=0) > 0
        logits = jnp.concatenate([jnp.dot(qbd, kp, preferred_element_type=F32) for kp in k_list], axis=1)
        lm = jnp.where(sel, logits, NEG_BIG)
        m_old = m_sc[:, 0:1]
        m_new = jnp.maximum(m_old, jnp.max(lm, axis=1, keepdims=True))
        p = jnp.exp(lm - m_new)
        alpha = jnp.exp(m_old - m_new)
        l_sc[...] = jnp.broadcast_to(alpha * l_sc[:, 0:1] + jnp.sum(p, axis=1, keepdims=True), (rows, LANES))
        pv = jnp.zeros((rows, 512), F32)
        for i, vp in enumerate(v_list):
            pv = pv + lax.dot_general(p[:, i * LANES:(i + 1) * LANES].astype(BF16), vp, (((1,), (1,)), ((), ())),
                                      preferred_element_type=F32)
        acc_sc[...] = alpha * acc_sc[...] + pv
        m_sc[...] = jnp.broadcast_to(m_new, (rows, LANES))

    @pl.when(j < n_steps)
    def _past():
        attend([r[0].astype(BF16) for r in k_refs], [r[0].astype(BF16) for r in v_refs],
               pl.multiple_of(j * (pages * PAGE_SIZE), pages * PAGE_SIZE))

    @pl.when(j == n_steps)
    def _new():
        attend([knew_ref[0]], [vnew_ref[0]], past)
        for h in range(N_HEADS):
            hs = slice(h * HEAD_DIM, (h + 1) * HEAD_DIM)
            rs = slice(h * nq, (h + 1) * nq)
            out_ref[0, :, hs] = (acc_sc[rs, hs] / l_sc[rs, 0:1]).astype(out_ref.dtype)


def _dsa_sample_attend(page_flat, qbd, keys, thr, jsel, knew, vnew, cache_k2, cache_v2, layer, n_pool, n_pages, nq):
    Bd = qbd.shape[0]
    pages = _pick_tile(n_pages, (8, 4, 2, 1))
    n_steps = n_pages // pages
    past = n_pages * PAGE_SIZE
    base = layer * n_pool
    rows = N_HEADS * nq

    def page_spec(p):
        return pl.BlockSpec((1, 512, PAGE_SIZE),
                            lambda b, j, pt: (base + pt[b * n_pages + jnp.minimum(j, n_steps - 1) * pages + p], 0, 0))

    per_b = lambda shape: pl.BlockSpec((1,) + shape, lambda b, j, pt: (b, 0, 0))
    grid_spec = pltpu.PrefetchScalarGridSpec(
        num_scalar_prefetch=1, grid=(Bd, n_steps + 1),
        in_specs=[per_b((rows, 512)), per_b((nq, past + LANES)), per_b((nq, LANES)), per_b((nq, LANES)),
                  per_b((512, LANES)), per_b((512, LANES))]
                 + [page_spec(p) for p in range(pages)] * 2,
        out_specs=per_b((nq, 512)),
        scratch_shapes=[pltpu.VMEM((rows, LANES), F32), pltpu.VMEM((rows, LANES), F32), pltpu.VMEM((rows, 512), F32)])
    return pl.pallas_call(
        functools.partial(_dsa_sample_attend_kernel, pages=pages, n_steps=n_steps, past=past, nq=nq),
        grid_spec=grid_spec, out_shape=jax.ShapeDtypeStruct((Bd, nq, 512), BF16),
        compiler_params=_cparams(("arbitrary", "arbitrary")), name="dsa_sample_attend",
    )(page_flat, qbd, keys, thr, jsel, knew, vnew, *([cache_k2] * pages), *([cache_v2] * pages))


def _log_sigmoid(x):
    return jnp.minimum(x, 0.0) - jnp.log1p(jnp.exp(-jnp.abs(x)))


def _mlstm_kernel(q_ref, k_ref, v_ref, misc_ref, gT_ref, c0_ref, n0_ref, m0_ref,
                  h_ref, c_out_ref, n_out_ref, m_out_ref, c_sc, n_sc, m_sc, *, lc, group):
    step = pl.program_id(1)
    hp = lax.Precision.HIGHEST

    @pl.when(step == 0)
    def _init():
        c_sc[...] = c0_ref[0]
        n_sc[...] = n0_ref[0]
        m_sc[...] = m0_ref[0]

    ti = lax.broadcasted_iota(I32, (lc, lc), 0)
    si = lax.broadcasted_iota(I32, (lc, lc), 1)
    tri = si <= ti
    for c in range(group):
        rs = slice(c * lc, (c + 1) * lc)
        for h in range(ML_HEADS):
            q = q_ref[rs, h * ML_QK_DIM:(h + 1) * ML_QK_DIM] * (ML_QK_DIM ** -0.5)
            k = k_ref[rs, h * ML_QK_DIM:(h + 1) * ML_QK_DIM]
            v = v_ref[rs, h * ML_V_DIM:(h + 1) * ML_V_DIM]
            i_col = misc_ref[rs, _M_MI + h:_M_MI + h + 1]
            lf_col = _log_sigmoid(misc_ref[rs, _M_MF + h:_M_MF + h + 1])
            i_row = gT_ref[0, h:h + 1, rs]
            lf_row = _log_sigmoid(gT_ref[0, ML_HEADS + h:ML_HEADS + h + 1, rs])
            m = m_sc[h][:, 0:1]
            b_col = jnp.sum(jnp.where(tri, lf_row, 0.0), axis=1, keepdims=True)
            b_row = jnp.sum(jnp.where(ti <= si, lf_col, 0.0), axis=0, keepdims=True)
            dmat = jnp.where(tri, b_col - b_row + i_row, -jnp.inf)
            inter = b_col + m
            mt = jnp.maximum(inter, jnp.max(dmat, axis=1, keepdims=True))
            s = lax.dot_general(q, k, (((1,), (1,)), ((), ())), preferred_element_type=F32, precision=hp)
            s = s * jnp.exp(dmat - mt)
            w = jnp.exp(inter - mt)
            num = (jnp.dot(s, v, preferred_element_type=F32, precision=hp)
                   + w * jnp.dot(q, c_sc[h], preferred_element_type=F32, precision=hp))
            den = jnp.sum(s, axis=1, keepdims=True) + w * jnp.sum(q * n_sc[h], axis=1, keepdims=True)
            h_ref[rs, h * ML_V_DIM:(h + 1) * ML_V_DIM] = num / jnp.maximum(jnp.abs(den), jnp.exp(-mt))
            bl = jnp.sum(lf_row, axis=1, keepdims=True)
            g_row = bl - b_row + i_row
            g_col = bl - b_col + i_col
            m_new = jnp.maximum(bl + m, jnp.max(g_row, axis=1, keepdims=True))
            decay = jnp.exp(bl + m - m_new)
            kw = jnp.exp(g_col - m_new) * k
            c_sc[h] = decay * c_sc[h] + lax.dot_general(kw, v, (((0,), (0,)), ((), ())),
                                                        preferred_element_type=F32, precision=hp)
            n_sc[h] = decay * n_sc[h] + jnp.sum(kw, axis=0, keepdims=True)
            m_sc[h] = jnp.broadcast_to(m_new, (1, LANES))

    @pl.when(step == pl.num_programs(1) - 1)
    def _finish():
        c_out_ref[0] = c_sc[...]
        n_out_ref[0] = n_sc[...]
        m_out_ref[0] = m_sc[...]


def _mlstm(q, k, v, misc, gT, c0, n0, m0, batch, length):
    lc = int(np.gcd(length, ML_CHUNK))
    nc = length // lc
    group = _pick_tile(nc, (4, 2, 1))
    if (group * lc) % LANES != 0 and nc != group:
        group = nc
    rows = group * lc
    steps = nc // group
    row = lambda w: pl.BlockSpec((rows, w), lambda b, s: (b * steps + s, 0))
    st = lambda shape: pl.BlockSpec((1,) + shape, lambda b, s: (b,) + (0,) * len(shape))
    n0 = n0.reshape(batch, ML_HEADS, 1, ML_QK_DIM)
    m0 = jnp.broadcast_to(m0.reshape(batch, ML_HEADS, 1, 1), (batch, ML_HEADS, 1, LANES))
    h, c, n, m = pl.pallas_call(
        functools.partial(_mlstm_kernel, lc=lc, group=group),
        grid=(batch, steps),
        in_specs=[row(256), row(256), row(512), row(LANES),
                  pl.BlockSpec((1, 2 * ML_HEADS, rows), lambda b, s: (b, 0, s)),
                  st((ML_HEADS, ML_QK_DIM, ML_V_DIM)), st((ML_HEADS, 1, ML_QK_DIM)), st((ML_HEADS, 1, LANES))],
        out_specs=[row(512), st((ML_HEADS, ML_QK_DIM, ML_V_DIM)), st((ML_HEADS, 1, ML_QK_DIM)),
                   st((ML_HEADS, 1, LANES))],
        out_shape=[jax.ShapeDtypeStruct((batch * length, 512), F32),
                   jax.ShapeDtypeStruct((batch, ML_HEADS, ML_QK_DIM, ML_V_DIM), F32),
                   jax.ShapeDtypeStruct((batch, ML_HEADS, 1, ML_QK_DIM), F32),
                   jax.ShapeDtypeStruct((batch, ML_HEADS, 1, LANES), F32)],
        scratch_shapes=[pltpu.VMEM((ML_HEADS, ML_QK_DIM, ML_V_DIM), F32), pltpu.VMEM((ML_HEADS, 1, ML_QK_DIM), F32),
                        pltpu.VMEM((ML_HEADS, 1, LANES), F32)],
        compiler_params=_cparams(("arbitrary", "arbitrary")), name="mlstm",
    )(q, k, v, misc, gT, c0, n0, m0)
    return h, c, n.reshape(batch, ML_HEADS, ML_QK_DIM), m[:, :, 0, 0]


def _merge_kernel(x_ref, a_ref, mh_ref, mo_ref, gates_ref, wb0_ref, wb1_ref, wout_ref, out_ref):
    m_out = (jax.nn.sigmoid(mo_ref[...]) * mh_ref[...]).astype(BF16)
    y0 = jnp.dot(a_ref[...], wb0_ref[...], preferred_element_type=F32)
    y1 = jnp.dot(m_out, wb1_ref[...], preferred_element_type=F32)
    merged = (jax.nn.sigmoid(gates_ref[:, :D_MODEL]) * y0 + jax.nn.sigmoid(gates_ref[:, D_MODEL:]) * y1)
    out_ref[...] = x_ref[...] + jnp.dot(merged.astype(BF16), wout_ref[...], preferred_element_type=F32)


def _merge(x, a_out, m_h, mo, gates, wb0, wb1, wout):
    R = x.shape[0]
    tm = _pick_tile(R, (512, 256, 128))
    row = lambda w: pl.BlockSpec((tm, w), lambda i: (i, 0))
    return pl.pallas_call(
        _merge_kernel, grid=(R // tm,),
        in_specs=[row(D_MODEL), row(512), row(512), row(512), row(2 * D_MODEL),
                  _const_spec(wb0.shape), _const_spec(wb1.shape), _const_spec(wout.shape)],
        out_specs=row(D_MODEL), out_shape=jax.ShapeDtypeStruct((R, D_MODEL), F32),
        compiler_params=_cparams(("parallel",)), name="merge",
    )(x, a_out, m_h, mo, gates, wb0, wb1, wout)


def _ffn_kernel(x_ref, g_ref, wgu_ref, wd_ref, out_ref, *, fc):
    x = x_ref[...]
    xn = (x * lax.rsqrt(jnp.mean(x * x, axis=-1, keepdims=True) + EPS) * g_ref[...]).astype(BF16)
    acc = x
    for f in range(0, FF_DIM, fc):
        gate = jnp.dot(xn, wgu_ref[:, f:f + fc], preferred_element_type=F32)
        up = jnp.dot(xn, wgu_ref[:, FF_DIM + f:FF_DIM + f + fc], preferred_element_type=F32)
        hidden = (gate * jax.nn.sigmoid(gate) * up).astype(BF16)
        acc = acc + jnp.dot(hidden, wd_ref[f:f + fc, :], preferred_element_type=F32)
    out_ref[...] = acc


def _ffn(x, g, wgu, wd):
    R = x.shape[0]
    tm = _pick_tile(R, (512, 256, 128))
    row = pl.BlockSpec((tm, D_MODEL), lambda i: (i, 0))
    return pl.pallas_call(
        functools.partial(_ffn_kernel, fc=256), grid=(R // tm,),
        in_specs=[row, _const_spec((1, D_MODEL)), _const_spec(wgu.shape), _const_spec(wd.shape)],
        out_specs=row, out_shape=jax.ShapeDtypeStruct((R, D_MODEL), F32),
        compiler_params=_cparams(("parallel",)), name="ffn",
    )(x, g, wgu, wd)


def _layer_weights(l, norm_mix, w_in, q_norm, k_norm, gate_bias, w_branch, w_out, norm_ffn, w_gate_up, w_down):
    w = w_in[l]
    wmain = jnp.concatenate([w[:, _C_AQ:_C_IK], w[:, _C_MQ:_C_MI], w[:, _C_MO:]], axis=1).astype(BF16)
    wmisc = jnp.concatenate([w[:, _C_IK:_C_MQ], w[:, _C_MI:_C_MO],
                             jnp.zeros((D_MODEL, LANES - 80), F32)], axis=1).astype(BF16)
    bias = jnp.concatenate([jnp.zeros((_M_MI,), F32), gate_bias[l], jnp.zeros((LANES - 80,), F32)])[None, :]
    head_id = np.arange(LANES) // HEAD_DIM
    bd = jnp.asarray((head_id[:, None] == head_id[None, :]).astype(np.float32) / HEAD_DIM)
    return dict(
        g_mix=norm_mix[l][None, :], wmain=wmain, wmisc=wmisc,
        qn=jnp.tile(q_norm[l], LANES // HEAD_DIM)[None, :], kn=jnp.tile(k_norm[l], LANES // HEAD_DIM)[None, :],
        bias=bias, bd=bd, wb0=w_branch[l, 0].astype(BF16), wb1=w_branch[l, 1].astype(BF16),
        wout=w_out[l].astype(BF16), g_ffn=norm_ffn[l][None, :], wgu=w_gate_up[l].astype(BF16),
        wd=w_down[l].astype(BF16))


def kernel(x_prompt, x_sample, cache_k, cache_v, cache_idx_k, state_C, state_n, state_m, page_table, norm_mix, w_in, q_norm, k_norm, gate_bias, w_branch, w_out, norm_ffn, w_gate_up, w_down):
    Bp, Lp, _ = x_prompt.shape
    Bd, Ld, _ = x_sample.shape
    depth = w_in.shape[0]
    n_pool = cache_k.shape[1]
    n_pages = page_table.shape[1]
    past = n_pages * PAGE_SIZE
    assert Bp == 1 and PAGE_SIZE == cache_k.shape[2]
    Rs = Bd * Ld

    xp = x_prompt.reshape(Lp, D_MODEL)
    xs = x_sample.reshape(Rs, D_MODEL)
    tab_p = _rope_tables(jnp.arange(Lp))
    tab_s = _rope_tables(jnp.tile(past + jnp.arange(Ld), Bd))
    cache_k2 = jnp.transpose(cache_k, (0, 1, 3, 4, 2)).reshape(depth * n_pool, 512, PAGE_SIZE)
    cache_v2 = jnp.transpose(cache_v, (0, 1, 3, 4, 2)).reshape(depth * n_pool, 512, PAGE_SIZE)
    cache_ik2 = jnp.transpose(cache_idx_k, (0, 1, 3, 2)).reshape(depth * n_pool, IDX_DIM, PAGE_SIZE)
    page_flat = page_table.reshape(-1).astype(I32)
    zeros_c = jnp.zeros((Bp, ML_HEADS, ML_QK_DIM, ML_V_DIM), F32)
    zeros_n = jnp.zeros((Bp, ML_HEADS, ML_QK_DIM), F32)
    zeros_m = jnp.zeros((Bp, ML_HEADS), F32)
    eye_h = jnp.eye(N_HEADS, dtype=BF16)

    outs = {k: [] for k in ("kp", "vp", "ikp", "Cp", "np", "mp", "ks", "vs", "iks", "Cs", "ns", "ms")}
    for l in range(depth):
        W = _layer_weights(l, norm_mix, w_in, q_norm, k_norm, gate_bias, w_branch, w_out, norm_ffn, w_gate_up, w_down)

        def inproj(x, tabs):
            return _inproj(x, tabs, W["g_mix"], W["wmain"], W["wmisc"], W["qn"], W["kn"], W["bias"], W["bd"])

        aq, akf, akb, avf, avb, iq, misc, ikT, mq, mk, mv, mo, gates = inproj(xp, tab_p)
        a_out = _dsa_prompt(aq, iq, misc, ikT, akb, avb)
        gT = jnp.transpose(misc[:, _M_MI:_M_MI + 2 * ML_HEADS])[None]
        m_h, C, n, m = _mlstm(mq, mk, mv, misc, gT, zeros_c, zeros_n, zeros_m, Bp, Lp)
        xp = _merge(xp, a_out, m_h, mo, gates, W["wb0"], W["wb1"], W["wout"])
        xp = _ffn(xp, W["g_ffn"], W["wgu"], W["wd"])
        outs["kp"].append(akf.reshape(Bp, Lp, N_HEADS, HEAD_DIM))
        outs["vp"].append(avf.reshape(Bp, Lp, N_HEADS, HEAD_DIM))
        outs["ikp"].append(misc[:, :IDX_DIM].reshape(Bp, Lp, IDX_DIM))
        outs["Cp"].append(C), outs["np"].append(n), outs["mp"].append(m)

        aq, akf, akb, avf, avb, iq, misc, ikT, mq, mk, mv, mo, gates = inproj(xs, tab_s)
        iqs = iq.reshape(Bd, Ld, N_IDX_HEADS, IDX_DIM).transpose(0, 2, 1, 3).reshape(Bd, N_IDX_HEADS * Ld, IDX_DIM)
        wst = misc[:, _M_IW:_M_IW + N_IDX_HEADS].reshape(Bd, Ld, N_IDX_HEADS).transpose(0, 2, 1)
        wst = jnp.broadcast_to(wst.reshape(Bd, N_IDX_HEADS * Ld, 1), (Bd, N_IDX_HEADS * Ld, LANES))
        pad_rows = lambda t: jnp.pad(t.reshape(Bd, Ld, -1).transpose(0, 2, 1), ((0, 0), (0, 0), (0, LANES - Ld)))
        iknew = pad_rows(misc[:, :IDX_DIM].astype(BF16))
        keys, thr, jsel = _dsa_sample_select(page_flat, iqs, wst, iknew, cache_ik2, l, n_pool, n_pages, Ld)
        qbd = jnp.einsum("bqhd,hg->bhqgd", aq.reshape(Bd, Ld, N_HEADS, HEAD_DIM), eye_h)
        qbd = qbd.reshape(Bd, N_HEADS * Ld, 512)
        a_out = _dsa_sample_attend(page_flat, qbd, keys, thr, jsel, pad_rows(akb), pad_rows(avb),
                                   cache_k2, cache_v2, l, n_pool, n_pages, Ld).reshape(Rs, 512)
        gT = misc[:, _M_MI:_M_MI + 2 * ML_HEADS].reshape(Bd, Ld, 2 * ML_HEADS).transpose(0, 2, 1)
        m_h, C, n, m = _mlstm(mq, mk, mv, misc, gT, state_C[l], state_n[l], state_m[l], Bd, Ld)
        xs = _merge(xs, a_out, m_h, mo, gates, W["wb0"], W["wb1"], W["wout"])
        xs = _ffn(xs, W["g_ffn"], W["wgu"], W["wd"])
        outs["ks"].append(akf.reshape(Bd, Ld, N_HEADS, HEAD_DIM))
        outs["vs"].append(avf.reshape(Bd, Ld, N_HEADS, HEAD_DIM))
        outs["iks"].append(misc[:, :IDX_DIM].reshape(Bd, Ld, IDX_DIM))
        outs["Cs"].append(C), outs["ns"].append(n), outs["ms"].append(m)

    st = lambda k: jnp.stack(outs[k])
    return (xp.reshape(Bp, Lp, D_MODEL), xs.reshape(Bd, Ld, D_MODEL),
            st("kp"), st("vp"), st("ikp"), st("Cp"), st("np"), st("mp"),
            st("ks"), st("vs"), st("iks"), st("Cs"), st("ns"), st("ms"))
```

```python
import functools

import numpy as np
import jax
import jax.numpy as jnp
from jax import lax
from jax.experimental import pallas as pl
from jax.experimental.pallas import tpu as pltpu

D_MODEL = 1024
BRANCH_WIDTH = D_MODEL // 2
HEAD_DIM = 64
N_HEADS = BRANCH_WIDTH // HEAD_DIM
N_IDX_HEADS = 8
IDX_DIM = 64
TOPK_MAX = 256
ROPE_THETA = 500000.0
ROPE_FRACTION = 4
ML_QK_DIM = 64
ML_V_DIM = 128
ML_HEADS = BRANCH_WIDTH // ML_V_DIM
ML_CHUNK = 64
FF_DIM = -(-8 * D_MODEL // (3 * 256)) * 256
PAGE_SIZE = 128
EPS = 1e-6

LANES = 128
VMEM_LIMIT = 56 * 1024 * 1024
INT_MIN = -2 ** 31
NEG_BIG = -1e30
CAND_DEPTH = 12
INSERT_ROWS = 32

_C_AQ, _C_IK, _C_IW, _C_MQ, _C_MI, _C_MO = 0, 2048, 2112, 2120, 3144, 3152
_IN_COLS = 5712
_M_IW, _M_MI, _M_MF = 64, 72, 76

F32 = jnp.float32
BF16 = jnp.bfloat16
I32 = jnp.int32


def _cparams(sem):
    return pltpu.CompilerParams(dimension_semantics=sem, vmem_limit_bytes=VMEM_LIMIT)


def _pick_tile(n, prefs):
    for t in prefs:
        if n % t == 0:
            return t
    return n


def _const_spec(shape):
    nd = len(shape)
    return pl.BlockSpec(shape, lambda *a: (0,) * nd)


def _inproj_kernel(x_ref, g_ref, wmain_ref, wmisc_ref, qn_ref, kn_ref, bias_ref, cos_ref, sin_ref, bd_ref,
                   aq_ref, akf_ref, akb_ref, avf_ref, avb_ref, iq_ref, misc_ref, ikT_ref,
                   mq_ref, mk_ref, mv_ref, mo_ref, gates_ref):
    x = x_ref[...]
    tm = x.shape[0]
    xn = (x * lax.rsqrt(jnp.mean(x * x, axis=-1, keepdims=True) + EPS) * g_ref[...]).astype(BF16)
    cosv = cos_ref[...]
    sinv = sin_ref[...]
    lane = lax.broadcasted_iota(I32, (tm, LANES), 1)
    first = (lane % HEAD_DIM) < (HEAD_DIM // ROPE_FRACTION // 2)
    shift = HEAD_DIM // ROPE_FRACTION // 2

    def rope(y, c, s):
        partner = jnp.where(first, pltpu.roll(y, LANES - shift, 1), pltpu.roll(y, shift, 1))
        return y * c + partner * s

    def headnorm(y, gain):
        ms = jnp.dot(y * y, bd_ref[...], preferred_element_type=F32, precision=lax.Precision.HIGHEST)
        return y * lax.rsqrt(ms + EPS) * gain

    def proj(c0, c1):
        return jnp.dot(xn, wmain_ref[:, c0:c1], preferred_element_type=F32)

    yq = proj(0, 512)
    yk = proj(512, 1024)
    yi = proj(1536, 2048)
    for g in range(4):
        sl = slice(g * LANES, (g + 1) * LANES)
        q = rope(headnorm(yq[:, sl], qn_ref[...]), cosv, sinv)
        aq_ref[:, sl] = (q * (HEAD_DIM ** -0.5)).astype(BF16)
        k = rope(headnorm(yk[:, sl], kn_ref[...]), cosv, sinv)
        akf_ref[:, sl] = k
        akb_ref[:, sl] = k.astype(BF16)
        iq_ref[:, sl] = (rope(yi[:, sl], cosv, sinv) * (IDX_DIM ** -0.5)).astype(BF16)
    yv = proj(1024, 1536)
    avf_ref[...] = yv
    avb_ref[...] = yv.astype(BF16)
    mq_ref[...] = proj(2048, 2304)
    mk_ref[...] = proj(2304, 2560)
    mv_ref[...] = proj(2560, 3072)
    mo_ref[...] = proj(3072, 3584)
    gates_ref[...] = proj(3584, 5632)
    ym = jnp.dot(xn, wmisc_ref[...], preferred_element_type=F32) + bias_ref[...]
    is_ik = lane < IDX_DIM
    ym = rope(ym, jnp.where(is_ik, cosv, 1.0), jnp.where(is_ik, sinv, 0.0))
    misc_ref[...] = ym
    ikT_ref[...] = jnp.transpose(ym)[:IDX_DIM, :].astype(BF16)


def _rope_tables(pos):
    half = HEAD_DIM // ROPE_FRACTION // 2
    freqs = ROPE_THETA ** (-jnp.arange(half, dtype=F32) / half)
    ang = pos.astype(F32)[:, None] * freqs[None, :]
    cos, sin = jnp.cos(ang), jnp.sin(ang)
    n = pos.shape[0]
    ones = jnp.ones((n, HEAD_DIM - 2 * half), F32)
    c = jnp.concatenate([cos, cos, ones], axis=1)
    s = jnp.concatenate([-sin, sin, 0.0 * ones], axis=1)
    return jnp.tile(c, (1, LANES // HEAD_DIM)), jnp.tile(s, (1, LANES // HEAD_DIM))


def _inproj(x, pos_tables, g, wmain, wmisc, qn, kn, bias, bd):
    R = x.shape[0]
    tm = _pick_tile(R, (256, 128))
    cos_t, sin_t = pos_tables
    row = lambda w: pl.BlockSpec((tm, w), lambda i: (i, 0))
    out_shapes = [
        jax.ShapeDtypeStruct((R, 512), BF16),
        jax.ShapeDtypeStruct((R, 512), F32),
        jax.ShapeDtypeStruct((R, 512), BF16),
        jax.ShapeDtypeStruct((R, 512), F32),
        jax.ShapeDtypeStruct((R, 512), BF16),
        jax.ShapeDtypeStruct((R, 512), BF16),
        jax.ShapeDtypeStruct((R, LANES), F32),
        jax.ShapeDtypeStruct((IDX_DIM, R), BF16),
        jax.ShapeDtypeStruct((R, 256), F32),
        jax.ShapeDtypeStruct((R, 256), F32),
        jax.ShapeDtypeStruct((R, 512), F32),
        jax.ShapeDtypeStruct((R, 512), F32),
        jax.ShapeDtypeStruct((R, 2048), F32),
    ]
    out_specs = [row(512), row(512), row(512), row(512), row(512), row(512), row(LANES),
                 pl.BlockSpec((IDX_DIM, tm), lambda i: (0, i)),
                 row(256), row(256), row(512), row(512), row(2048)]
    in_specs = [row(D_MODEL), _const_spec((1, D_MODEL)), _const_spec(wmain.shape), _const_spec(wmisc.shape),
                _const_spec((1, LANES)), _const_spec((1, LANES)), _const_spec((1, LANES)),
                row(LANES), row(LANES), _const_spec((LANES, LANES))]
    return pl.pallas_call(
        _inproj_kernel, grid=(R // tm,), in_specs=in_specs, out_specs=out_specs, out_shape=out_shapes,
        compiler_params=_cparams(("parallel",)), name="inproj",
    )(x, g, wmain, wmisc, qn, kn, bias, cos_t, sin_t, bd)


def _f32_key(x):
    bits = pltpu.bitcast(x, I32)
    return bits ^ (lax.shift_right_arithmetic(bits, 31) & 0x7FFFFFFF)


def _key_f32(k):
    return pltpu.bitcast(k ^ (lax.shift_right_arithmetic(k, 31) & 0x7FFFFFFF), F32)


def _count_rows(ref, r0, rows, n_iter, inner, preds, as_key=False):
    def body(c, accs):
        accs = list(accs)
        for u in range(inner):
            k0 = pl.multiple_of((c * inner + u) * LANES, LANES)
            kk = ref[r0:r0 + rows, pl.ds(k0, LANES)]
            if as_key:
                kk = _f32_key(kk)
            for i, pred in enumerate(preds):
                accs[i] = accs[i] + pred(kk, k0).astype(I32)
        return tuple(accs)
    zero = jnp.zeros((rows, LANES), I32)
    if isinstance(n_iter, int) and n_iter <= 16:
        accs = (zero,) * len(preds)
        for c in range(n_iter):
            accs = body(c, accs)
    else:
        accs = lax.fori_loop(0, n_iter, body, (zero,) * len(preds))
    return [jnp.sum(a, axis=1, keepdims=True) for a in accs]


def _kth_largest(ref, blocks, rows, n_iter, inner, as_key=False):
    def bit_pass(bi, t_us):
        bit = lax.shift_left(jnp.int32(1), 31 - bi)
        out = []
        for (r0, kprime), t_u in zip(blocks, t_us):
            cand = jnp.broadcast_to((t_u | bit) ^ INT_MIN, (rows, LANES))
            cnt, = _count_rows(ref, r0, rows, n_iter, inner, [lambda kk, k0, cand=cand: kk >= cand], as_key)
            out.append(jnp.where(cnt >= kprime, t_u | bit, t_u))
        return tuple(out)

    t_us = lax.fori_loop(0, 32, bit_pass, tuple(jnp.zeros((rows, 1), I32) for _ in blocks))
    return [t_u ^ INT_MIN for t_u in t_us]


def _tie_cutoff(sc_ref, r0, rows, n_iter, inner, thr, need, n_idx_bits):
    thr_b = jnp.broadcast_to(thr, (rows, LANES))
    lane = lax.broadcasted_iota(I32, (rows, LANES), 1)

    def idx_pass(bi, j):
        bit = lax.shift_left(jnp.int32(1), n_idx_bits - 1 - bi)
        cand = jnp.broadcast_to(j | bit, (rows, LANES))
        cnt, = _count_rows(sc_ref, r0, rows, n_iter, inner,
                           [lambda kk, k0: (kk == thr_b) & ((lane + k0) < cand)])
        return jnp.where(cnt < need, j | bit, j)

    return lax.fori_loop(0, n_idx_bits, idx_pass, jnp.zeros((rows, 1), I32))


def _lane_top_candidates(sc_ref, cand_ref, r0, rows, n_iter, inner, depth):
    sub = INSERT_ROWS if rows % INSERT_ROWS == 0 else 8

    def row_block(rb, carry):
        rs = pl.multiple_of(r0 + rb * sub, sub)

        def body(c, tops):
            tops = list(tops)
            for u in range(inner):
                k0 = pl.multiple_of((c * inner + u) * LANES, LANES)
                x = sc_ref[pl.ds(rs, sub), pl.ds(k0, LANES)]
                for i in range(depth):
                    hi = jnp.maximum(tops[i], x)
                    x = jnp.minimum(tops[i], x)
                    tops[i] = hi
            return tuple(tops)

        tops = lax.fori_loop(0, n_iter, body, (jnp.full((sub, LANES), -jnp.inf, F32),) * depth)
        for i in range(depth):
            cand_ref[pl.ds(rs, sub), i * LANES:(i + 1) * LANES] = _f32_key(tops[i])
        return carry

    lax.fori_loop(0, rows // sub, row_block, 0)


def _topk_select(sc_ref, cand_ref, thr_ref, jsel_ref, blocks, rows, n_iter, inner, n_idx_bits):
    for r0, _ in blocks:
        _lane_top_candidates(sc_ref, cand_ref, r0, rows, n_iter, inner, CAND_DEPTH)
    thrs = _kth_largest(cand_ref, blocks, rows, CAND_DEPTH, 1)
    for (r0, kprime), thr_key in zip(blocks, thrs):
        thr_b = jnp.broadcast_to(_key_f32(thr_key), (rows, LANES))
        n_gt, n_ge = _count_rows(sc_ref, r0, rows, n_iter, inner,
                                 [lambda kk, k0: kk > thr_b, lambda kk, k0: kk >= thr_b])
        thr_ref[r0:r0 + rows, :] = thr_b
        jsel_ref[r0:r0 + rows, :] = jnp.full((rows, LANES), 2 ** 31 - 1, I32)
        redo = jnp.max(((n_gt >= kprime) | (n_ge != kprime)).astype(I32))

        @pl.when(redo > 0)
        def _full_search():
            t_key, = _kth_largest(sc_ref, [(r0, kprime)], rows, n_iter, inner, as_key=True)
            t = _key_f32(t_key)
            t_b = jnp.broadcast_to(t, (rows, LANES))
            gt, = _count_rows(sc_ref, r0, rows, n_iter, inner, [lambda kk, k0: kk > t_b])
            j_sel = _tie_cutoff(sc_ref, r0, rows, n_iter, inner, t, kprime - gt, n_idx_bits)
            thr_ref[r0:r0 + rows, :] = t_b
            jsel_ref[r0:r0 + rows, :] = jnp.broadcast_to(j_sel, (rows, LANES))


def _dsa_prompt_kernel(qi_ref, kj_ref, last_ref,
                       aq_ref, iq_ref, misc_ref, ikT_ref, k_ref, v_ref, out_ref,
                       sc_sc, cand_sc, thr_sc, jsel_sc, qpair_sc, bias_sc, lm_sc, alpha_sc, m_sc, acc_sc,
                       *, tq, kc, topk, seq_len):
    s_id = pl.program_id(0)
    qi = qi_ref[s_id]
    kj = kj_ref[s_id]
    n_idx_bits = int(seq_len).bit_length()
    n_pairs = N_HEADS // 2
    n_cb = kc // LANES

    @pl.when(kj == 0)
    def _select():
        q_pos = qi * tq + lax.broadcasted_iota(I32, (tq, 1), 0)
        n_kc = (qi * tq + tq + kc - 1) // kc
        w_all = misc_ref[:, _M_IW:_M_IW + N_IDX_HEADS] * (N_IDX_HEADS ** -0.5)

        def score_body(c, carry):
            k0 = pl.multiple_of(c * kc, kc)
            ikc = ikT_ref[:, pl.ds(k0, kc)]
            acc = jnp.zeros((tq, kc), F32)
            for h in range(N_IDX_HEADS):
                s = jnp.dot(iq_ref[:, h * IDX_DIM:(h + 1) * IDX_DIM], ikc, preferred_element_type=F32)
                acc = acc + jnp.maximum(s, 0.0) * w_all[:, h:h + 1]
            k_pos = k0 + lax.broadcasted_iota(I32, (1, kc), 1)
            sc_sc[:, pl.ds(k0, kc)] = jnp.where(k_pos <= q_pos, acc, -jnp.inf)
            return carry

        lax.fori_loop(0, n_kc, score_body, 0)
        kprime = jnp.minimum(topk, q_pos + 1)
        rb = min(tq, LANES)
        _topk_select(sc_sc, cand_sc, thr_sc, jsel_sc, [(r0, kprime[r0:r0 + rb]) for r0 in range(0, tq, rb)],
                     rb, n_kc, kc // LANES, n_idx_bits)
        pair_half = lax.broadcasted_iota(I32, (tq, LANES), 1) // HEAD_DIM
        for j in range(n_pairs):
            q2 = aq_ref[:, j * LANES:(j + 1) * LANES]
            for half in range(2):
                qpair_sc[j, half * tq:(half + 1) * tq, :] = jnp.where(pair_half == half, q2, jnp.zeros_like(q2))
        m_sc[...] = jnp.full(m_sc.shape, NEG_BIG, F32)
        acc_sc[...] = jnp.zeros(acc_sc.shape, F32)

    k0 = pl.multiple_of(kj * kc, kc)
    thr = thr_sc[...]
    j_sel = jsel_sc[...]
    lane = lax.broadcasted_iota(I32, (1, LANES), 1)
    for c in range(n_cb):
        kk = sc_sc[:, pl.ds(k0 + c * LANES, LANES)]
        sel = (kk > thr) | ((kk == thr) & ((k0 + c * LANES + lane) <= j_sel))
        bias_sc[:, c * LANES:(c + 1) * LANES] = jnp.where(sel, 0.0, NEG_BIG)

    for j in range(n_pairs):
        s = lax.dot_general(qpair_sc[j], k_ref[:, j * LANES:(j + 1) * LANES], (((1,), (1,)), ((), ())),
                            preferred_element_type=F32)
        for half in range(2):
            h = 2 * j + half
            lm = [s[half * tq:(half + 1) * tq, c * LANES:(c + 1) * LANES] + bias_sc[:, c * LANES:(c + 1) * LANES]
                  for c in range(n_cb)]
            mx = lm[0]
            for c in range(1, n_cb):
                mx = jnp.maximum(mx, lm[c])
            m_old = m_sc[h]
            m_new = jnp.maximum(m_old, jnp.max(mx, axis=1, keepdims=True))
            for c in range(n_cb):
                lm_sc[h, :, c * LANES:(c + 1) * LANES] = lm[c] - m_new
            alpha_sc[h] = jnp.exp(m_old - m_new)
            m_sc[h] = m_new

    ones = jnp.ones((kc, LANES), BF16)
    for j in range(n_pairs):
        p = jnp.exp(lm_sc[2 * j:2 * j + 2].reshape(2 * tq, kc)).astype(BF16)
        v_aug = jnp.concatenate([v_ref[:, j * LANES:(j + 1) * LANES], ones], axis=1)
        pv = jnp.dot(p, v_aug, preferred_element_type=F32)
        for half in range(2):
            h = 2 * j + half
            alpha = alpha_sc[h]
            acc_sc[h] = jnp.concatenate([alpha, alpha], axis=1) * acc_sc[h] + pv[half * tq:(half + 1) * tq]

    @pl.when(last_ref[s_id] == 1)
    def _finish():
        for h in range(N_HEADS):
            own = slice((h % 2) * HEAD_DIM, (h % 2 + 1) * HEAD_DIM)
            den = slice(LANES + (h % 2) * HEAD_DIM, LANES + (h % 2 + 1) * HEAD_DIM)
            out_ref[:, h * HEAD_DIM:(h + 1) * HEAD_DIM] = (acc_sc[h][:, own] / acc_sc[h][:, den]).astype(out_ref.dtype)


def _dsa_prompt(aq, iq, misc, ikT, akb, avb):
    L = aq.shape[0]
    tq = _pick_tile(L, (256, 128))
    kc = _pick_tile(L, (512, 256, 128))
    topk = min(TOPK_MAX, L // 4)
    qi, kj, last = [], [], []
    for i in range(L // tq):
        n_kc = ((i + 1) * tq + kc - 1) // kc
        for j in range(n_kc):
            qi.append(i), kj.append(j), last.append(int(j == n_kc - 1))
    qi, kj, last = (jnp.asarray(np.asarray(a, np.int32)) for a in (qi, kj, last))
    qrow = lambda w: pl.BlockSpec((tq, w), lambda s, qi, kj, last: (qi[s], 0))
    krow = pl.BlockSpec((kc, 512), lambda s, qi, kj, last: (kj[s], 0))
    grid_spec = pltpu.PrefetchScalarGridSpec(
        num_scalar_prefetch=3, grid=(int(qi.shape[0]),),
        in_specs=[qrow(512), qrow(512), qrow(LANES), pl.BlockSpec((IDX_DIM, L), lambda s, qi, kj, last: (0, 0)),
                  krow, krow],
        out_specs=qrow(512),
        scratch_shapes=[pltpu.VMEM((tq, L), F32), pltpu.VMEM((tq, CAND_DEPTH * LANES), I32),
                        pltpu.VMEM((tq, LANES), F32), pltpu.VMEM((tq, LANES), I32),
                        pltpu.VMEM((N_HEADS // 2, 2 * tq, LANES), BF16), pltpu.VMEM((tq, kc), F32),
                        pltpu.VMEM((N_HEADS, tq, kc), F32), pltpu.VMEM((N_HEADS, tq, LANES), F32),
                        pltpu.VMEM((N_HEADS, tq, LANES), F32), pltpu.VMEM((N_HEADS, tq, 2 * LANES), F32)])
    return pl.pallas_call(
        functools.partial(_dsa_prompt_kernel, tq=tq, kc=kc, topk=topk, seq_len=L),
        grid_spec=grid_spec, out_shape=jax.ShapeDtypeStruct((L, 512), BF16),
        compiler_params=_cparams(("arbitrary",)), name="dsa_prompt",
    )(qi, kj, last, aq, iq, misc, ikT, akb, avb)


def _dsa_sample_select_kernel(pt_ref, iqs_ref, wst_ref, iknew_ref, *rest, pages, n_pages, topk, past, nq, batch):
    page_refs = rest[:pages]
    keys_ref, thr_ref, jsel_ref, cand_sc = rest[pages:pages + 4]
    b = pl.program_id(0)
    j = pl.program_id(1)
    last_j = n_pages // pages - 1
    iqs = iqs_ref[0]
    w = wst_ref[0][:, 0:1] * (N_IDX_HEADS ** -0.5)
    row0 = pl.multiple_of(b * nq, nq)

    def scores(ik_t):
        s = jnp.dot(iqs, ik_t, preferred_element_type=F32)
        t = jnp.maximum(s, 0.0) * w
        acc = jnp.zeros((nq, ik_t.shape[1]), F32)
        for h in range(N_IDX_HEADS):
            acc = acc + t[h * nq:(h + 1) * nq, :]
        return acc

    k0 = pl.multiple_of(j * (pages * PAGE_SIZE), pages * PAGE_SIZE)
    ik_pages = jnp.concatenate([r[0] for r in page_refs], axis=1).astype(BF16)
    keys_ref[pl.ds(row0, nq), pl.ds(k0, pages * PAGE_SIZE)] = scores(ik_pages)

    @pl.when(j == last_j)
    def _new_tokens():
        qrow = lax.broadcasted_iota(I32, (nq, LANES), 0)
        lane = lax.broadcasted_iota(I32, (nq, LANES), 1)
        keys_ref[pl.ds(row0, nq), past:past + LANES] = jnp.where(lane <= qrow, scores(iknew_ref[0]), -jnp.inf)

    @pl.when((j == last_j) & (b == batch - 1))
    def _finish():
        rows = batch * nq
        rb = min(rows, LANES)
        qrow = lax.broadcasted_iota(I32, (rb, 1), 0) % nq
        _topk_select(keys_ref, cand_sc, thr_ref, jsel_ref,
                     [(r0, jnp.minimum(topk, past + 1 + qrow)) for r0 in range(0, rows, rb)],
                     rb, n_pages + 1, 1, int(past + LANES).bit_length())


def _dsa_sample_select(page_flat, iqs, wst, iknew, cache_ik2, layer, n_pool, n_pages, nq):
    Bd = iqs.shape[0]
    pages = _pick_tile(n_pages, (16, 8, 4, 2, 1))
    past = n_pages * PAGE_SIZE
    topk = min(TOPK_MAX, (past + nq) // 4)
    base = layer * n_pool
    rows = Bd * nq

    def page_spec(p):
        return pl.BlockSpec((1, IDX_DIM, PAGE_SIZE),
                            lambda b, j, pt: (base + pt[b * n_pages + j * pages + p], 0, 0))

    per_b = lambda shape: pl.BlockSpec((1,) + shape, lambda b, j, pt: (b, 0, 0))
    whole = lambda shape: pl.BlockSpec(shape, lambda b, j, pt: (0, 0))
    grid_spec = pltpu.PrefetchScalarGridSpec(
        num_scalar_prefetch=1, grid=(Bd, n_pages // pages),
        in_specs=[per_b((N_IDX_HEADS * nq, IDX_DIM)), per_b((N_IDX_HEADS * nq, LANES)), per_b((IDX_DIM, LANES))]
                 + [page_spec(p) for p in range(pages)],
        out_specs=[whole((rows, past + LANES)), whole((rows, LANES)), whole((rows, LANES))],
        scratch_shapes=[pltpu.VMEM((rows, CAND_DEPTH * LANES), I32)])
    return pl.pallas_call(
        functools.partial(_dsa_sample_select_kernel, pages=pages, n_pages=n_pages, topk=topk, past=past, nq=nq,
                          batch=Bd),
        grid_spec=grid_spec,
        out_shape=[jax.ShapeDtypeStruct((rows, past + LANES), F32), jax.ShapeDtypeStruct((rows, LANES), F32),
                   jax.ShapeDtypeStruct((rows, LANES), I32)],
        compiler_params=_cparams(("arbitrary", "arbitrary")), name="dsa_sample_select",
    )(page_flat, iqs, wst, iknew, *([cache_ik2] * pages))


def _dsa_sample_attend_kernel(pt_ref, qbd_ref, keys_ref, thr_ref, jsel_ref, knew_ref, vnew_ref, *rest,
                              pages, n_steps, past, nq):
    k_refs = rest[:pages]
    v_refs = rest[pages:2 * pages]
    out_ref = rest[2 * pages]
    m_sc, l_sc, acc_sc = rest[2 * pages + 1:]
    j = pl.program_id(1)
    rows = N_HEADS * nq
    qbd = qbd_ref[0]
    thr = thr_ref[:, 0:1]
    j_sel = jsel_ref[:, 0:1]

    @pl.when(j == 0)
    def _init():
        m_sc[...] = jnp.full(m_sc.shape, NEG_BIG, F32)
        l_sc[...] = jnp.zeros(l_sc.shape, F32)
        acc_sc[...] = jnp.zeros(acc_sc.shape, F32)

    def attend(k_list, v_list, k0):
        width = LANES * len(k_list)
        kk = keys_ref[:, pl.ds(k0, width)]
        k_pos = k0 + lax.broadcasted_iota(I32, (1, width), 1)
        sel = (kk > thr) | ((kk == thr) & (k_pos <= j_sel))
        sel = jnp.concatenate([sel.astype(I32)] * N_HEADS, axis=0) > 0
        logits = jnp.concatenate([jnp.dot(qbd, kp, preferred_element_type=F32) for kp in k_list], axis=1)
        lm = jnp.where(sel, logits, NEG_BIG)
        m_old = m_sc[:, 0:1]
        m_new = jnp.maximum(m_old, jnp.max(lm, axis=1, keepdims=True))
        p = jnp.exp(lm - m_new)
        alpha = jnp.exp(m_old - m_new)
        l_sc[...] = jnp.broadcast_to(alpha * l_sc[:, 0:1] + jnp.sum(p, axis=1, keepdims=True), (rows, LANES))
        pv = jnp.zeros((rows, 512), F32)
        for i, vp in enumerate(v_list):
            pv = pv + lax.dot_general(p[:, i * LANES:(i + 1) * LANES].astype(BF16), vp, (((1,), (1,)), ((), ())),
                                      preferred_element_type=F32)
        acc_sc[...] = alpha * acc_sc[...] + pv
        m_sc[...] = jnp.broadcast_to(m_new, (rows, LANES))

    @pl.when(j < n_steps)
    def _past():
        attend([r[0].astype(BF16) for r in k_refs], [r[0].astype(BF16) for r in v_refs],
               pl.multiple_of(j * (pages * PAGE_SIZE), pages * PAGE_SIZE))

    @pl.when(j == n_steps)
    def _new():
        attend([knew_ref[0]], [vnew_ref[0]], past)
        for h in range(N_HEADS):
            hs = slice(h * HEAD_DIM, (h + 1) * HEAD_DIM)
            rs = slice(h * nq, (h + 1) * nq)
            out_ref[0, :, hs] = (acc_sc[rs, hs] / l_sc[rs, 0:1]).astype(out_ref.dtype)


def _dsa_sample_attend(page_flat, qbd, keys, thr, jsel, knew, vnew, cache_k2, cache_v2, layer, n_pool, n_pages, nq):
    Bd = qbd.shape[0]
    pages = _pick_tile(n_pages, (8, 4, 2, 1))
    n_steps = n_pages // pages
    past = n_pages * PAGE_SIZE
    base = layer * n_pool
    rows = N_HEADS * nq

    def page_spec(p):
        return pl.BlockSpec((1, 512, PAGE_SIZE),
                            lambda b, j, pt: (base + pt[b * n_pages + jnp.minimum(j, n_steps - 1) * pages + p], 0, 0))

    per_b = lambda shape: pl.BlockSpec((1,) + shape, lambda b, j, pt: (b, 0, 0))
    per_q = lambda w: pl.BlockSpec((nq, w), lambda b, j, pt: (b, 0))
    grid_spec = pltpu.PrefetchScalarGridSpec(
        num_scalar_prefetch=1, grid=(Bd, n_steps + 1),
        in_specs=[per_b((rows, 512)), per_q(past + LANES), per_q(LANES), per_q(LANES),
                  per_b((512, LANES)), per_b((512, LANES))]
                 + [page_spec(p) for p in range(pages)] * 2,
        out_specs=per_b((nq, 512)),
        scratch_shapes=[pltpu.VMEM((rows, LANES), F32), pltpu.VMEM((rows, LANES), F32), pltpu.VMEM((rows, 512), F32)])
    return pl.pallas_call(
        functools.partial(_dsa_sample_attend_kernel, pages=pages, n_steps=n_steps, past=past, nq=nq),
        grid_spec=grid_spec, out_shape=jax.ShapeDtypeStruct((Bd, nq, 512), BF16),
        compiler_params=_cparams(("arbitrary", "arbitrary")), name="dsa_sample_attend",
    )(page_flat, qbd, keys, thr, jsel, knew, vnew, *([cache_k2] * pages), *([cache_v2] * pages))


def _log_sigmoid(x):
    return jnp.minimum(x, 0.0) - jnp.log1p(jnp.exp(-jnp.abs(x)))


def _mlstm_kernel(q_ref, k_ref, v_ref, misc_ref, gT_ref, c0_ref, n0_ref, m0_ref,
                  h_ref, c_out_ref, n_out_ref, m_out_ref, c_sc, n_sc, m_sc, *, lc, group):
    step = pl.program_id(1)
    hp = lax.Precision.HIGHEST

    @pl.when(step == 0)
    def _init():
        c_sc[...] = c0_ref[0]
        n_sc[...] = n0_ref[0]
        m_sc[...] = m0_ref[0]

    ti = lax.broadcasted_iota(I32, (lc, lc), 0)
    si = lax.broadcasted_iota(I32, (lc, lc), 1)
    tri = si <= ti
    for c in range(group):
        rs = slice(c * lc, (c + 1) * lc)
        for h in range(ML_HEADS):
            q = q_ref[rs, h * ML_QK_DIM:(h + 1) * ML_QK_DIM] * (ML_QK_DIM ** -0.5)
            k = k_ref[rs, h * ML_QK_DIM:(h + 1) * ML_QK_DIM]
            v = v_ref[rs, h * ML_V_DIM:(h + 1) * ML_V_DIM]
            i_col = misc_ref[rs, _M_MI + h:_M_MI + h + 1]
            lf_col = _log_sigmoid(misc_ref[rs, _M_MF + h:_M_MF + h + 1])
            i_row = gT_ref[0, h:h + 1, rs]
            lf_row = _log_sigmoid(gT_ref[0, ML_HEADS + h:ML_HEADS + h + 1, rs])
            m = m_sc[h][:, 0:1]
            b_col = jnp.sum(jnp.where(tri, lf_row, 0.0), axis=1, keepdims=True)
            b_row = jnp.sum(jnp.where(ti <= si, lf_col, 0.0), axis=0, keepdims=True)
            dmat = jnp.where(tri, b_col - b_row + i_row, -jnp.inf)
            inter = b_col + m
            mt = jnp.maximum(inter, jnp.max(dmat, axis=1, keepdims=True))
            s = lax.dot_general(q, k, (((1,), (1,)), ((), ())), preferred_element_type=F32, precision=hp)
            s = s * jnp.exp(dmat - mt)
            w = jnp.exp(inter - mt)
            num = (jnp.dot(s, v, preferred_element_type=F32, precision=hp)
                   + w * jnp.dot(q, c_sc[h], preferred_element_type=F32, precision=hp))
            den = jnp.sum(s, axis=1, keepdims=True) + w * jnp.sum(q * n_sc[h], axis=1, keepdims=True)
            h_ref[rs, h * ML_V_DIM:(h + 1) * ML_V_DIM] = num / jnp.maximum(jnp.abs(den), jnp.exp(-mt))
            bl = jnp.sum(lf_row, axis=1, keepdims=True)
            g_row = bl - b_row + i_row
            g_col = bl - b_col + i_col
            m_new = jnp.maximum(bl + m, jnp.max(g_row, axis=1, keepdims=True))
            decay = jnp.exp(bl + m - m_new)
            kw = jnp.exp(g_col - m_new) * k
            c_sc[h] = decay * c_sc[h] + lax.dot_general(kw, v, (((0,), (0,)), ((), ())),
                                                        preferred_element_type=F32, precision=hp)
            n_sc[h] = decay * n_sc[h] + jnp.sum(kw, axis=0, keepdims=True)
            m_sc[h] = jnp.broadcast_to(m_new, (1, LANES))

    @pl.when(step == pl.num_programs(1) - 1)
    def _finish():
        c_out_ref[0] = c_sc[...]
        n_out_ref[0] = n_sc[...]
        m_out_ref[0] = m_sc[...]


def _mlstm(q, k, v, misc, gT, c0, n0, m0, batch, length):
    lc = int(np.gcd(length, ML_CHUNK))
    nc = length // lc
    group = _pick_tile(nc, (4, 2, 1))
    if (group * lc) % LANES != 0 and nc != group:
        group = nc
    rows = group * lc
    steps = nc // group
    row = lambda w: pl.BlockSpec((rows, w), lambda b, s: (b * steps + s, 0))
    st = lambda shape: pl.BlockSpec((1,) + shape, lambda b, s: (b,) + (0,) * len(shape))
    n0 = n0.reshape(batch, ML_HEADS, 1, ML_QK_DIM)
    m0 = jnp.broadcast_to(m0.reshape(batch, ML_HEADS, 1, 1), (batch, ML_HEADS, 1, LANES))
    h, c, n, m = pl.pallas_call(
        functools.partial(_mlstm_kernel, lc=lc, group=group),
        grid=(batch, steps),
        in_specs=[row(256), row(256), row(512), row(LANES),
                  pl.BlockSpec((1, 2 * ML_HEADS, rows), lambda b, s: (b, 0, s)),
                  st((ML_HEADS, ML_QK_DIM, ML_V_DIM)), st((ML_HEADS, 1, ML_QK_DIM)), st((ML_HEADS, 1, LANES))],
        out_specs=[row(512), st((ML_HEADS, ML_QK_DIM, ML_V_DIM)), st((ML_HEADS, 1, ML_QK_DIM)),
                   st((ML_HEADS, 1, LANES))],
        out_shape=[jax.ShapeDtypeStruct((batch * length, 512), F32),
                   jax.ShapeDtypeStruct((batch, ML_HEADS, ML_QK_DIM, ML_V_DIM), F32),
                   jax.ShapeDtypeStruct((batch, ML_HEADS, 1, ML_QK_DIM), F32),
                   jax.ShapeDtypeStruct((batch, ML_HEADS, 1, LANES), F32)],
        scratch_shapes=[pltpu.VMEM((ML_HEADS, ML_QK_DIM, ML_V_DIM), F32), pltpu.VMEM((ML_HEADS, 1, ML_QK_DIM), F32),
                        pltpu.VMEM((ML_HEADS, 1, LANES), F32)],
        compiler_params=_cparams(("arbitrary", "arbitrary")), name="mlstm",
    )(q, k, v, misc, gT, c0, n0, m0)
    return h, c, n.reshape(batch, ML_HEADS, ML_QK_DIM), m[:, :, 0, 0]


def _merge_kernel(x_ref, a_ref, mh_ref, mo_ref, gates_ref, wb0_ref, wb1_ref, wout_ref, out_ref):
    m_out = (jax.nn.sigmoid(mo_ref[...]) * mh_ref[...]).astype(BF16)
    y0 = jnp.dot(a_ref[...], wb0_ref[...], preferred_element_type=F32)
    y1 = jnp.dot(m_out, wb1_ref[...], preferred_element_type=F32)
    merged = (jax.nn.sigmoid(gates_ref[:, :D_MODEL]) * y0 + jax.nn.sigmoid(gates_ref[:, D_MODEL:]) * y1)
    out_ref[...] = x_ref[...] + jnp.dot(merged.astype(BF16), wout_ref[...], preferred_element_type=F32)


def _merge(x, a_out, m_h, mo, gates, wb0, wb1, wout):
    R = x.shape[0]
    tm = _pick_tile(R, (512, 256, 128))
    row = lambda w: pl.BlockSpec((tm, w), lambda i: (i, 0))
    return pl.pallas_call(
        _merge_kernel, grid=(R // tm,),
        in_specs=[row(D_MODEL), row(512), row(512), row(512), row(2 * D_MODEL),
                  _const_spec(wb0.shape), _const_spec(wb1.shape), _const_spec(wout.shape)],
        out_specs=row(D_MODEL), out_shape=jax.ShapeDtypeStruct((R, D_MODEL), F32),
        compiler_params=_cparams(("parallel",)), name="merge",
    )(x, a_out, m_h, mo, gates, wb0, wb1, wout)


def _ffn_kernel(x_ref, g_ref, wgu_ref, wd_ref, out_ref, *, fc):
    x = x_ref[...]
    xn = (x * lax.rsqrt(jnp.mean(x * x, axis=-1, keepdims=True) + EPS) * g_ref[...]).astype(BF16)
    acc = x
    for f in range(0, FF_DIM, fc):
        gate = jnp.dot(xn, wgu_ref[:, f:f + fc], preferred_element_type=F32)
        up = jnp.dot(xn, wgu_ref[:, FF_DIM + f:FF_DIM + f + fc], preferred_element_type=F32)
        hidden = (gate * jax.nn.sigmoid(gate) * up).astype(BF16)
        acc = acc + jnp.dot(hidden, wd_ref[f:f + fc, :], preferred_element_type=F32)
    out_ref[...] = acc


def _ffn(x, g, wgu, wd):
    R = x.shape[0]
    tm = _pick_tile(R, (512, 256, 128))
    row = pl.BlockSpec((tm, D_MODEL), lambda i: (i, 0))
    return pl.pallas_call(
        functools.partial(_ffn_kernel, fc=256), grid=(R // tm,),
        in_specs=[row, _const_spec((1, D_MODEL)), _const_spec(wgu.shape), _const_spec(wd.shape)],
        out_specs=row, out_shape=jax.ShapeDtypeStruct((R, D_MODEL), F32),
        compiler_params=_cparams(("parallel",)), name="ffn",
    )(x, g, wgu, wd)


def _layer_weights(l, norm_mix, w_in, q_norm, k_norm, gate_bias, w_branch, w_out, norm_ffn, w_gate_up, w_down):
    w = w_in[l]
    wmain = jnp.concatenate([w[:, _C_AQ:_C_IK], w[:, _C_MQ:_C_MI], w[:, _C_MO:]], axis=1).astype(BF16)
    wmisc = jnp.concatenate([w[:, _C_IK:_C_MQ], w[:, _C_MI:_C_MO],
                             jnp.zeros((D_MODEL, LANES - 80), F32)], axis=1).astype(BF16)
    bias = jnp.concatenate([jnp.zeros((_M_MI,), F32), gate_bias[l], jnp.zeros((LANES - 80,), F32)])[None, :]
    head_id = np.arange(LANES) // HEAD_DIM
    bd = jnp.asarray((head_id[:, None] == head_id[None, :]).astype(np.float32) / HEAD_DIM)
    return dict(
        g_mix=norm_mix[l][None, :], wmain=wmain, wmisc=wmisc,
        qn=jnp.tile(q_norm[l], LANES // HEAD_DIM)[None, :], kn=jnp.tile(k_norm[l], LANES // HEAD_DIM)[None, :],
        bias=bias, bd=bd, wb0=w_branch[l, 0].astype(BF16), wb1=w_branch[l, 1].astype(BF16),
        wout=w_out[l].astype(BF16), g_ffn=norm_ffn[l][None, :], wgu=w_gate_up[l].astype(BF16),
        wd=w_down[l].astype(BF16))


def kernel(x_prompt, x_sample, cache_k, cache_v, cache_idx_k, state_C, state_n, state_m, page_table, norm_mix, w_in, q_norm, k_norm, gate_bias, w_branch, w_out, norm_ffn, w_gate_up, w_down):
    Bp, Lp, _ = x_prompt.shape
    Bd, Ld, _ = x_sample.shape
    depth = w_in.shape[0]
    n_pool = cache_k.shape[1]
    n_pages = page_table.shape[1]
    past = n_pages * PAGE_SIZE
    assert Bp == 1 and PAGE_SIZE == cache_k.shape[2]
    Rs = Bd * Ld

    xp = x_prompt.reshape(Lp, D_MODEL)
    xs = x_sample.reshape(Rs, D_MODEL)
    tab_p = _rope_tables(jnp.arange(Lp))
    tab_s = _rope_tables(jnp.tile(past + jnp.arange(Ld), Bd))
    cache_k2 = jnp.transpose(cache_k, (0, 1, 3, 4, 2)).reshape(depth * n_pool, 512, PAGE_SIZE)
    cache_v2 = jnp.transpose(cache_v, (0, 1, 3, 4, 2)).reshape(depth * n_pool, 512, PAGE_SIZE)
    cache_ik2 = jnp.transpose(cache_idx_k, (0, 1, 3, 2)).reshape(depth * n_pool, IDX_DIM, PAGE_SIZE)
    page_flat = page_table.reshape(-1).astype(I32)
    zeros_c = jnp.zeros((Bp, ML_HEADS, ML_QK_DIM, ML_V_DIM), F32)
    zeros_n = jnp.zeros((Bp, ML_HEADS, ML_QK_DIM), F32)
    zeros_m = jnp.zeros((Bp, ML_HEADS), F32)
    eye_h = jnp.eye(N_HEADS, dtype=BF16)

    outs = {k: [] for k in ("kp", "vp", "ikp", "Cp", "np", "mp", "ks", "vs", "iks", "Cs", "ns", "ms")}
    for l in range(depth):
        W = _layer_weights(l, norm_mix, w_in, q_norm, k_norm, gate_bias, w_branch, w_out, norm_ffn, w_gate_up, w_down)

        def inproj(x, tabs):
            return _inproj(x, tabs, W["g_mix"], W["wmain"], W["wmisc"], W["qn"], W["kn"], W["bias"], W["bd"])

        aq, akf, akb, avf, avb, iq, misc, ikT, mq, mk, mv, mo, gates = inproj(xp, tab_p)
        a_out = _dsa_prompt(aq, iq, misc, ikT, akb, avb)
        gT = jnp.transpose(misc[:, _M_MI:_M_MI + 2 * ML_HEADS])[None]
        m_h, C, n, m = _mlstm(mq, mk, mv, misc, gT, zeros_c, zeros_n, zeros_m, Bp, Lp)
        xp = _merge(xp, a_out, m_h, mo, gates, W["wb0"], W["wb1"], W["wout"])
        xp = _ffn(xp, W["g_ffn"], W["wgu"], W["wd"])
        outs["kp"].append(akf.reshape(Bp, Lp, N_HEADS, HEAD_DIM))
        outs["vp"].append(avf.reshape(Bp, Lp, N_HEADS, HEAD_DIM))
        outs["ikp"].append(misc[:, :IDX_DIM].reshape(Bp, Lp, IDX_DIM))
        outs["Cp"].append(C), outs["np"].append(n), outs["mp"].append(m)

        aq, akf, akb, avf, avb, iq, misc, ikT, mq, mk, mv, mo, gates = inproj(xs, tab_s)
        iqs = iq.reshape(Bd, Ld, N_IDX_HEADS, IDX_DIM).transpose(0, 2, 1, 3).reshape(Bd, N_IDX_HEADS * Ld, IDX_DIM)
        wst = misc[:, _M_IW:_M_IW + N_IDX_HEADS].reshape(Bd, Ld, N_IDX_HEADS).transpose(0, 2, 1)
        wst = jnp.broadcast_to(wst.reshape(Bd, N_IDX_HEADS * Ld, 1), (Bd, N_IDX_HEADS * Ld, LANES))
        pad_rows = lambda t: jnp.pad(t.reshape(Bd, Ld, -1).transpose(0, 2, 1), ((0, 0), (0, 0), (0, LANES - Ld)))
        iknew = pad_rows(misc[:, :IDX_DIM].astype(BF16))
        keys, thr, jsel = _dsa_sample_select(page_flat, iqs, wst, iknew, cache_ik2, l, n_pool, n_pages, Ld)
        qbd = jnp.einsum("bqhd,hg->bhqgd", aq.reshape(Bd, Ld, N_HEADS, HEAD_DIM), eye_h)
        qbd = qbd.reshape(Bd, N_HEADS * Ld, 512)
        a_out = _dsa_sample_attend(page_flat, qbd, keys, thr, jsel, pad_rows(akb), pad_rows(avb),
                                   cache_k2, cache_v2, l, n_pool, n_pages, Ld).reshape(Rs, 512)
        gT = misc[:, _M_MI:_M_MI + 2 * ML_HEADS].reshape(Bd, Ld, 2 * ML_HEADS).transpose(0, 2, 1)
        m_h, C, n, m = _mlstm(mq, mk, mv, misc, gT, state_C[l], state_n[l], state_m[l], Bd, Ld)
        xs = _merge(xs, a_out, m_h, mo, gates, W["wb0"], W["wb1"], W["wout"])
        xs = _ffn(xs, W["g_ffn"], W["wgu"], W["wd"])
        outs["ks"].append(akf.reshape(Bd, Ld, N_HEADS, HEAD_DIM))
        outs["vs"].append(avf.reshape(Bd, Ld, N_HEADS, HEAD_DIM))
        outs["iks"].append(misc[:, :IDX_DIM].reshape(Bd, Ld, IDX_DIM))
        outs["Cs"].append(C), outs["ns"].append(n), outs["ms"].append(m)

    st = lambda k: jnp.stack(outs[k])
    return (xp.reshape(Bp, Lp, D_MODEL), xs.reshape(Bd, Ld, D_MODEL),
            st("kp"), st("vp"), st("ikp"), st("Cp"), st("np"), st("mp"),
            st("ks"), st("vs"), st("iks"), st("Cs"), st("ns"), st("ms"))
```

```python
import functools

import numpy as np
import jax
import jax.numpy as jnp
from jax import lax
from jax.experimental import pallas as pl
from jax.experimental.pallas import tpu as pltpu

D_MODEL = 1024
BRANCH_WIDTH = D_MODEL // 2
HEAD_DIM = 64
N_HEADS = BRANCH_WIDTH // HEAD_DIM
N_IDX_HEADS = 8
IDX_DIM = 64
TOPK_MAX = 256
ROPE_THETA = 500000.0
ROPE_FRACTION = 4
ML_QK_DIM = 64
ML_V_DIM = 128
ML_HEADS = BRANCH_WIDTH // ML_V_DIM
ML_CHUNK = 64
FF_DIM = -(-8 * D_MODEL // (3 * 256)) * 256
PAGE_SIZE = 128
EPS = 1e-6

LANES = 128
VMEM_LIMIT = 56 * 1024 * 1024
INT_MIN = -2 ** 31
NEG_BIG = -1e30
CAND_DEPTH = 12
INSERT_ROWS = 32

_C_AQ, _C_IK, _C_IW, _C_MQ, _C_MI, _C_MO = 0, 2048, 2112, 2120, 3144, 3152
_IN_COLS = 5712
_M_IW, _M_MI, _M_MF = 64, 72, 76

F32 = jnp.float32
BF16 = jnp.bfloat16
I32 = jnp.int32


def _cparams(sem):
    return pltpu.CompilerParams(dimension_semantics=sem, vmem_limit_bytes=VMEM_LIMIT)


def _pick_tile(n, prefs):
    for t in prefs:
        if n % t == 0:
            return t
    return n


def _const_spec(shape):
    nd = len(shape)
    return pl.BlockSpec(shape, lambda *a: (0,) * nd)


def _inproj_kernel(x_ref, g_ref, wmain_ref, wmisc_ref, qn_ref, kn_ref, bias_ref, cos_ref, sin_ref, bd_ref,
                   aq_ref, akf_ref, akb_ref, avf_ref, avb_ref, iq_ref, misc_ref, ikT_ref,
                   mq_ref, mk_ref, mv_ref, mo_ref, gates_ref):
    x = x_ref[...]
    tm = x.shape[0]
    xn = (x * lax.rsqrt(jnp.mean(x * x, axis=-1, keepdims=True) + EPS) * g_ref[...]).astype(BF16)
    cosv = cos_ref[...]
    sinv = sin_ref[...]
    lane = lax.broadcasted_iota(I32, (tm, LANES), 1)
    first = (lane % HEAD_DIM) < (HEAD_DIM // ROPE_FRACTION // 2)
    shift = HEAD_DIM // ROPE_FRACTION // 2

    def rope(y, c, s):
        partner = jnp.where(first, pltpu.roll(y, LANES - shift, 1), pltpu.roll(y, shift, 1))
        return y * c + partner * s

    def headnorm(y, gain):
        ms = jnp.dot(y * y, bd_ref[...], preferred_element_type=F32, precision=lax.Precision.HIGHEST)
        return y * lax.rsqrt(ms + EPS) * gain

    def proj(c0, c1):
        return jnp.dot(xn, wmain_ref[:, c0:c1], preferred_element_type=F32)

    yq = proj(0, 512)
    yk = proj(512, 1024)
    yi = proj(1536, 2048)
    for g in range(4):
        sl = slice(g * LANES, (g + 1) * LANES)
        q = rope(headnorm(yq[:, sl], qn_ref[...]), cosv, sinv)
        aq_ref[:, sl] = (q * (HEAD_DIM ** -0.5)).astype(BF16)
        k = rope(headnorm(yk[:, sl], kn_ref[...]), cosv, sinv)
        akf_ref[:, sl] = k
        akb_ref[:, sl] = k.astype(BF16)
        iq_ref[:, sl] = (rope(yi[:, sl], cosv, sinv) * (IDX_DIM ** -0.5)).astype(BF16)
    yv = proj(1024, 1536)
    avf_ref[...] = yv
    avb_ref[...] = yv.astype(BF16)
    mq_ref[...] = proj(2048, 2304)
    mk_ref[...] = proj(2304, 2560)
    mv_ref[...] = proj(2560, 3072)
    mo_ref[...] = proj(3072, 3584)
    gates_ref[...] = proj(3584, 5632)
    ym = jnp.dot(xn, wmisc_ref[...], preferred_element_type=F32) + bias_ref[...]
    is_ik = lane < IDX_DIM
    ym = rope(ym, jnp.where(is_ik, cosv, 1.0), jnp.where(is_ik, sinv, 0.0))
    misc_ref[...] = ym
    ikT_ref[...] = jnp.transpose(ym)[:IDX_DIM, :].astype(BF16)


def _rope_tables(pos):
    half = HEAD_DIM // ROPE_FRACTION // 2
    freqs = ROPE_THETA ** (-jnp.arange(half, dtype=F32) / half)
    ang = pos.astype(F32)[:, None] * freqs[None, :]
    cos, sin = jnp.cos(ang), jnp.sin(ang)
    n = pos.shape[0]
    ones = jnp.ones((n, HEAD_DIM - 2 * half), F32)
    c = jnp.concatenate([cos, cos, ones], axis=1)
    s = jnp.concatenate([-sin, sin, 0.0 * ones], axis=1)
    return jnp.tile(c, (1, LANES // HEAD_DIM)), jnp.tile(s, (1, LANES // HEAD_DIM))


def _inproj(x, pos_tables, g, wmain, wmisc, qn, kn, bias, bd):
    R = x.shape[0]
    tm = _pick_tile(R, (256, 128))
    cos_t, sin_t = pos_tables
    row = lambda w: pl.BlockSpec((tm, w), lambda i: (i, 0))
    out_shapes = [
        jax.ShapeDtypeStruct((R, 512), BF16),
        jax.ShapeDtypeStruct((R, 512), F32),
        jax.ShapeDtypeStruct((R, 512), BF16),
        jax.ShapeDtypeStruct((R, 512), F32),
        jax.ShapeDtypeStruct((R, 512), BF16),
        jax.ShapeDtypeStruct((R, 512), BF16),
        jax.ShapeDtypeStruct((R, LANES), F32),
        jax.ShapeDtypeStruct((IDX_DIM, R), BF16),
        jax.ShapeDtypeStruct((R, 256), F32),
        jax.ShapeDtypeStruct((R, 256), F32),
        jax.ShapeDtypeStruct((R, 512), F32),
        jax.ShapeDtypeStruct((R, 512), F32),
        jax.ShapeDtypeStruct((R, 2048), F32),
    ]
    out_specs = [row(512), row(512), row(512), row(512), row(512), row(512), row(LANES),
                 pl.BlockSpec((IDX_DIM, tm), lambda i: (0, i)),
                 row(256), row(256), row(512), row(512), row(2048)]
    in_specs = [row(D_MODEL), _const_spec((1, D_MODEL)), _const_spec(wmain.shape), _const_spec(wmisc.shape),
                _const_spec((1, LANES)), _const_spec((1, LANES)), _const_spec((1, LANES)),
                row(LANES), row(LANES), _const_spec((LANES, LANES))]
    return pl.pallas_call(
        _inproj_kernel, grid=(R // tm,), in_specs=in_specs, out_specs=out_specs, out_shape=out_shapes,
        compiler_params=_cparams(("parallel",)), name="inproj",
    )(x, g, wmain, wmisc, qn, kn, bias, cos_t, sin_t, bd)


def _f32_key(x):
    bits = pltpu.bitcast(x, I32)
    return bits ^ (lax.shift_right_arithmetic(bits, 31) & 0x7FFFFFFF)


def _key_f32(k):
    return pltpu.bitcast(k ^ (lax.shift_right_arithmetic(k, 31) & 0x7FFFFFFF), F32)


def _count_rows(ref, r0, rows, n_iter, inner, preds, as_key=False):
    def body(c, accs):
        accs = list(accs)
        for u in range(inner):
            k0 = pl.multiple_of((c * inner + u) * LANES, LANES)
            kk = ref[r0:r0 + rows, pl.ds(k0, LANES)]
            if as_key:
                kk = _f32_key(kk)
            for i, pred in enumerate(preds):
                accs[i] = accs[i] + pred(kk, k0).astype(I32)
        return tuple(accs)
    zero = jnp.zeros((rows, LANES), I32)
    if isinstance(n_iter, int) and n_iter <= 16:
        accs = (zero,) * len(preds)
        for c in range(n_iter):
            accs = body(c, accs)
    else:
        accs = lax.fori_loop(0, n_iter, body, (zero,) * len(preds))
    return [jnp.sum(a, axis=1, keepdims=True) for a in accs]


def _kth_largest(ref, blocks, rows, n_iter, inner, as_key=False):
    def bit_pass(bi, t_us):
        bit = lax.shift_left(jnp.int32(1), 31 - bi)
        out = []
        for (r0, kprime), t_u in zip(blocks, t_us):
            cand = jnp.broadcast_to((t_u | bit) ^ INT_MIN, (rows, LANES))
            cnt, = _count_rows(ref, r0, rows, n_iter, inner, [lambda kk, k0, cand=cand: kk >= cand], as_key)
            out.append(jnp.where(cnt >= kprime, t_u | bit, t_u))
        return tuple(out)

    t_us = lax.fori_loop(0, 32, bit_pass, tuple(jnp.zeros((rows, 1), I32) for _ in blocks))
    return [t_u ^ INT_MIN for t_u in t_us]


def _tie_cutoff(sc_ref, r0, rows, n_iter, inner, thr, need, n_idx_bits):
    thr_b = jnp.broadcast_to(thr, (rows, LANES))
    lane = lax.broadcasted_iota(I32, (rows, LANES), 1)

    def idx_pass(bi, j):
        bit = lax.shift_left(jnp.int32(1), n_idx_bits - 1 - bi)
        cand = jnp.broadcast_to(j | bit, (rows, LANES))
        cnt, = _count_rows(sc_ref, r0, rows, n_iter, inner,
                           [lambda kk, k0: (kk == thr_b) & ((lane + k0) < cand)])
        return jnp.where(cnt < need, j | bit, j)

    return lax.fori_loop(0, n_idx_bits, idx_pass, jnp.zeros((rows, 1), I32))


def _tie_walk(sc_ref, r0, rows, n_iter, inner, thr_b, need, max_need):
    lane = lax.broadcasted_iota(I32, (rows, LANES), 1)
    far = jnp.float32(3e38)

    def step(t, j_cur):
        j_b = jnp.broadcast_to(j_cur, (rows, LANES))

        def body(c, acc):
            for u in range(inner):
                k0 = pl.multiple_of((c * inner + u) * LANES, LANES)
                idx = (lane + k0).astype(F32)
                hit = (sc_ref[r0:r0 + rows, pl.ds(k0, LANES)] == thr_b) & (idx > j_b)
                acc = jnp.minimum(acc, jnp.where(hit, idx, far))
            return acc

        acc = lax.fori_loop(0, n_iter, body, jnp.full((rows, LANES), far, F32))
        return jnp.where(t < need, jnp.min(acc, axis=1, keepdims=True), j_cur)

    return lax.fori_loop(0, max_need, step, jnp.full((rows, 1), -1.0, F32)).astype(I32)


def _lane_top_candidates(sc_ref, cand_ref, r0, rows, n_iter, inner, depth):
    sub = INSERT_ROWS if rows % INSERT_ROWS == 0 else 8

    def row_block(rb, carry):
        rs = pl.multiple_of(r0 + rb * sub, sub)

        def body(c, tops):
            tops = list(tops)
            for u in range(inner):
                k0 = pl.multiple_of((c * inner + u) * LANES, LANES)
                x = sc_ref[pl.ds(rs, sub), pl.ds(k0, LANES)]
                for i in range(depth):
                    hi = jnp.maximum(tops[i], x)
                    x = jnp.minimum(tops[i], x)
                    tops[i] = hi
            return tuple(tops)

        tops = lax.fori_loop(0, n_iter, body, (jnp.full((sub, LANES), -jnp.inf, F32),) * depth)
        for i in range(depth):
            cand_ref[pl.ds(rs, sub), i * LANES:(i + 1) * LANES] = _f32_key(tops[i])
        return carry

    lax.fori_loop(0, rows // sub, row_block, 0)


def _topk_select(sc_ref, cand_ref, thr_ref, jsel_ref, blocks, rows, n_iter, inner, n_idx_bits):
    for r0, _ in blocks:
        _lane_top_candidates(sc_ref, cand_ref, r0, rows, n_iter, inner, CAND_DEPTH)
    thrs = _kth_largest(cand_ref, blocks, rows, CAND_DEPTH, 1)
    for (r0, kprime), thr_key in zip(blocks, thrs):
        thr_b = jnp.broadcast_to(_key_f32(thr_key), (rows, LANES))
        n_gt, n_ge = _count_rows(sc_ref, r0, rows, n_iter, inner,
                                 [lambda kk, k0: kk > thr_b, lambda kk, k0: kk >= thr_b])
        thr_ref[r0:r0 + rows, :] = thr_b
        jsel_ref[r0:r0 + rows, :] = jnp.full((rows, LANES), 2 ** 31 - 1, I32)
        wrong = jnp.max(((n_gt >= kprime) | (n_ge < kprime)).astype(I32))
        excess = n_ge > kprime
        need = jnp.where(excess, kprime - n_gt, 0)
        max_need = jnp.max(need)

        @pl.when((wrong == 0) & (max_need > 0))
        def _ties_only():
            @pl.when(max_need <= n_idx_bits)
            def _walk():
                j_sel = _tie_walk(sc_ref, r0, rows, n_iter, inner, thr_b, need, max_need)
                jsel_ref[r0:r0 + rows, :] = jnp.broadcast_to(jnp.where(excess, j_sel, 2 ** 31 - 1), (rows, LANES))

            @pl.when(max_need > n_idx_bits)
            def _bisect():
                j_sel = _tie_cutoff(sc_ref, r0, rows, n_iter, inner, thr_b[:, 0:1], kprime - n_gt, n_idx_bits)
                jsel_ref[r0:r0 + rows, :] = jnp.broadcast_to(jnp.where(excess, j_sel, 2 ** 31 - 1), (rows, LANES))

        @pl.when(wrong > 0)
        def _full_search():
            t_key, = _kth_largest(sc_ref, [(r0, kprime)], rows, n_iter, inner, as_key=True)
            t = _key_f32(t_key)
            t_b = jnp.broadcast_to(t, (rows, LANES))
            gt, = _count_rows(sc_ref, r0, rows, n_iter, inner, [lambda kk, k0: kk > t_b])
            j_sel = _tie_cutoff(sc_ref, r0, rows, n_iter, inner, t, kprime - gt, n_idx_bits)
            thr_ref[r0:r0 + rows, :] = t_b
            jsel_ref[r0:r0 + rows, :] = jnp.broadcast_to(j_sel, (rows, LANES))


def _dsa_prompt_kernel(qi_ref, kj_ref, last_ref,
                       aq_ref, iq_ref, misc_ref, ikT_ref, k_ref, v_ref, out_ref,
                       sc_sc, cand_sc, thr_sc, jsel_sc, qpair_sc, bias_sc, lm_sc, alpha_sc, m_sc, acc_sc,
                       *, tq, kc, topk, seq_len):
    s_id = pl.program_id(0)
    qi = qi_ref[s_id]
    kj = kj_ref[s_id]
    n_idx_bits = int(seq_len).bit_length()
    n_pairs = N_HEADS // 2
    n_cb = kc // LANES

    @pl.when(kj == 0)
    def _select():
        q_pos = qi * tq + lax.broadcasted_iota(I32, (tq, 1), 0)
        n_kc = (qi * tq + tq + kc - 1) // kc
        w_all = misc_ref[:, _M_IW:_M_IW + N_IDX_HEADS] * (N_IDX_HEADS ** -0.5)

        def score_body(c, carry):
            k0 = pl.multiple_of(c * kc, kc)
            ikc = ikT_ref[:, pl.ds(k0, kc)]
            acc = jnp.zeros((tq, kc), F32)
            for h in range(N_IDX_HEADS):
                s = jnp.dot(iq_ref[:, h * IDX_DIM:(h + 1) * IDX_DIM], ikc, preferred_element_type=F32)
                acc = acc + jnp.maximum(s, 0.0) * w_all[:, h:h + 1]
            k_pos = k0 + lax.broadcasted_iota(I32, (1, kc), 1)
            sc_sc[:, pl.ds(k0, kc)] = jnp.where(k_pos <= q_pos, acc, -jnp.inf)
            return carry

        lax.fori_loop(0, n_kc, score_body, 0)
        kprime = jnp.minimum(topk, q_pos + 1)
        rb = min(tq, LANES)
        _topk_select(sc_sc, cand_sc, thr_sc, jsel_sc, [(r0, kprime[r0:r0 + rb]) for r0 in range(0, tq, rb)],
                     rb, n_kc, kc // LANES, n_idx_bits)
        pair_half = lax.broadcasted_iota(I32, (tq, LANES), 1) // HEAD_DIM
        for j in range(n_pairs):
            q2 = aq_ref[:, j * LANES:(j + 1) * LANES]
            for half in range(2):
                qpair_sc[j, half * tq:(half + 1) * tq, :] = jnp.where(pair_half == half, q2, jnp.zeros_like(q2))
        m_sc[...] = jnp.full(m_sc.shape, NEG_BIG, F32)
        acc_sc[...] = jnp.zeros(acc_sc.shape, F32)

    k0 = pl.multiple_of(kj * kc, kc)
    thr = thr_sc[...]
    j_sel = jsel_sc[...]
    lane = lax.broadcasted_iota(I32, (1, LANES), 1)
    for c in range(n_cb):
        kk = sc_sc[:, pl.ds(k0 + c * LANES, LANES)]
        sel = (kk > thr) | ((kk == thr) & ((k0 + c * LANES + lane) <= j_sel))
        bias_sc[:, c * LANES:(c + 1) * LANES] = jnp.where(sel, 0.0, NEG_BIG)

    for j in range(n_pairs):
        s = lax.dot_general(qpair_sc[j], k_ref[:, j * LANES:(j + 1) * LANES], (((1,), (1,)), ((), ())),
                            preferred_element_type=F32)
        for half in range(2):
            h = 2 * j + half
            lm = [s[half * tq:(half + 1) * tq, c * LANES:(c + 1) * LANES] + bias_sc[:, c * LANES:(c + 1) * LANES]
                  for c in range(n_cb)]
            mx = lm[0]
            for c in range(1, n_cb):
                mx = jnp.maximum(mx, lm[c])
            m_old = m_sc[h]
            m_new = jnp.maximum(m_old, jnp.max(mx, axis=1, keepdims=True))
            for c in range(n_cb):
                lm_sc[h, :, c * LANES:(c + 1) * LANES] = lm[c] - m_new
            alpha_sc[h] = jnp.exp(m_old - m_new)
            m_sc[h] = m_new

    ones = jnp.ones((kc, LANES), BF16)
    for j in range(n_pairs):
        p = jnp.exp(lm_sc[2 * j:2 * j + 2].reshape(2 * tq, kc)).astype(BF16)
        v_aug = jnp.concatenate([v_ref[:, j * LANES:(j + 1) * LANES], ones], axis=1)
        pv = jnp.dot(p, v_aug, preferred_element_type=F32)
        for half in range(2):
            h = 2 * j + half
            alpha = alpha_sc[h]
            acc_sc[h] = jnp.concatenate([alpha, alpha], axis=1) * acc_sc[h] + pv[half * tq:(half + 1) * tq]

    @pl.when(last_ref[s_id] == 1)
    def _finish():
        for h in range(N_HEADS):
            own = slice((h % 2) * HEAD_DIM, (h % 2 + 1) * HEAD_DIM)
            den = slice(LANES + (h % 2) * HEAD_DIM, LANES + (h % 2 + 1) * HEAD_DIM)
            out_ref[:, h * HEAD_DIM:(h + 1) * HEAD_DIM] = (acc_sc[h][:, own] / acc_sc[h][:, den]).astype(out_ref.dtype)


def _dsa_prompt(aq, iq, misc, ikT, akb, avb):
    L = aq.shape[0]
    tq = _pick_tile(L, (256, 128))
    kc = _pick_tile(L, (512, 256, 128))
    topk = min(TOPK_MAX, L // 4)
    qi, kj, last = [], [], []
    for i in range(L // tq):
        n_kc = ((i + 1) * tq + kc - 1) // kc
        for j in range(n_kc):
            qi.append(i), kj.append(j), last.append(int(j == n_kc - 1))
    qi, kj, last = (jnp.asarray(np.asarray(a, np.int32)) for a in (qi, kj, last))
    qrow = lambda w: pl.BlockSpec((tq, w), lambda s, qi, kj, last: (qi[s], 0))
    krow = pl.BlockSpec((kc, 512), lambda s, qi, kj, last: (kj[s], 0))
    grid_spec = pltpu.PrefetchScalarGridSpec(
        num_scalar_prefetch=3, grid=(int(qi.shape[0]),),
        in_specs=[qrow(512), qrow(512), qrow(LANES), pl.BlockSpec((IDX_DIM, L), lambda s, qi, kj, last: (0, 0)),
                  krow, krow],
        out_specs=qrow(512),
        scratch_shapes=[pltpu.VMEM((tq, L), F32), pltpu.VMEM((tq, CAND_DEPTH * LANES), I32),
                        pltpu.VMEM((tq, LANES), F32), pltpu.VMEM((tq, LANES), I32),
                        pltpu.VMEM((N_HEADS // 2, 2 * tq, LANES), BF16), pltpu.VMEM((tq, kc), F32),
                        pltpu.VMEM((N_HEADS, tq, kc), F32), pltpu.VMEM((N_HEADS, tq, LANES), F32),
                        pltpu.VMEM((N_HEADS, tq, LANES), F32), pltpu.VMEM((N_HEADS, tq, 2 * LANES), F32)])
    return pl.pallas_call(
        functools.partial(_dsa_prompt_kernel, tq=tq, kc=kc, topk=topk, seq_len=L),
        grid_spec=grid_spec, out_shape=jax.ShapeDtypeStruct((L, 512), BF16),
        compiler_params=_cparams(("arbitrary",)), name="dsa_prompt",
    )(qi, kj, last, aq, iq, misc, ikT, akb, avb)


def _dsa_sample_select_kernel(pt_ref, iqs_ref, wst_ref, iknew_ref, *rest, pages, n_pages, topk, past, nq, batch):
    page_refs = rest[:pages]
    keys_ref, thr_ref, jsel_ref, cand_sc = rest[pages:pages + 4]
    b = pl.program_id(0)
    j = pl.program_id(1)
    last_j = n_pages // pages - 1
    iqs = iqs_ref[0]
    w = wst_ref[0][:, 0:1] * (N_IDX_HEADS ** -0.5)
    row0 = pl.multiple_of(b * nq, nq)

    def scores(ik_t):
        s = jnp.dot(iqs, ik_t, preferred_element_type=F32)
        t = jnp.maximum(s, 0.0) * w
        acc = jnp.zeros((nq, ik_t.shape[1]), F32)
        for h in range(N_IDX_HEADS):
            acc = acc + t[h * nq:(h + 1) * nq, :]
        return acc

    k0 = pl.multiple_of(j * (pages * PAGE_SIZE), pages * PAGE_SIZE)
    ik_pages = jnp.concatenate([r[0] for r in page_refs], axis=1).astype(BF16)
    keys_ref[pl.ds(row0, nq), pl.ds(k0, pages * PAGE_SIZE)] = scores(ik_pages)

    @pl.when(j == last_j)
    def _new_tokens():
        qrow = lax.broadcasted_iota(I32, (nq, LANES), 0)
        lane = lax.broadcasted_iota(I32, (nq, LANES), 1)
        keys_ref[pl.ds(row0, nq), past:past + LANES] = jnp.where(lane <= qrow, scores(iknew_ref[0]), -jnp.inf)

    @pl.when((j == last_j) & (b == batch - 1))
    def _finish():
        rows = batch * nq
        rb = min(rows, LANES)
        qrow = lax.broadcasted_iota(I32, (rb, 1), 0) % nq
        _topk_select(keys_ref, cand_sc, thr_ref, jsel_ref,
                     [(r0, jnp.minimum(topk, past + 1 + qrow)) for r0 in range(0, rows, rb)],
                     rb, n_pages + 1, 1, int(past + LANES).bit_length())


def _dsa_sample_select(page_flat, iqs, wst, iknew, cache_ik2, layer, n_pool, n_pages, nq):
    Bd = iqs.shape[0]
    pages = _pick_tile(n_pages, (16, 8, 4, 2, 1))
    past = n_pages * PAGE_SIZE
    topk = min(TOPK_MAX, (past + nq) // 4)
    base = layer * n_pool
    rows = Bd * nq

    def page_spec(p):
        return pl.BlockSpec((1, IDX_DIM, PAGE_SIZE),
                            lambda b, j, pt: (base + pt[b * n_pages + j * pages + p], 0, 0))

    per_b = lambda shape: pl.BlockSpec((1,) + shape, lambda b, j, pt: (b, 0, 0))
    whole = lambda shape: pl.BlockSpec(shape, lambda b, j, pt: (0, 0))
    grid_spec = pltpu.PrefetchScalarGridSpec(
        num_scalar_prefetch=1, grid=(Bd, n_pages // pages),
        in_specs=[per_b((N_IDX_HEADS * nq, IDX_DIM)), per_b((N_IDX_HEADS * nq, LANES)), per_b((IDX_DIM, LANES))]
                 + [page_spec(p) for p in range(pages)],
        out_specs=[whole((rows, past + LANES)), whole((rows, LANES)), whole((rows, LANES))],
        scratch_shapes=[pltpu.VMEM((rows, CAND_DEPTH * LANES), I32)])
    return pl.pallas_call(
        functools.partial(_dsa_sample_select_kernel, pages=pages, n_pages=n_pages, topk=topk, past=past, nq=nq,
                          batch=Bd),
        grid_spec=grid_spec,
        out_shape=[jax.ShapeDtypeStruct((rows, past + LANES), F32), jax.ShapeDtypeStruct((rows, LANES), F32),
                   jax.ShapeDtypeStruct((rows, LANES), I32)],
        compiler_params=_cparams(("arbitrary", "arbitrary")), name="dsa_sample_select",
    )(page_flat, iqs, wst, iknew, *([cache_ik2] * pages))


def _dsa_sample_attend_kernel(pt_ref, qbd_ref, keys_ref, thr_ref, jsel_ref, knew_ref, vnew_ref, *rest,
                              pages, n_steps, past, nq):
    k_refs = rest[:pages]
    v_refs = rest[pages:2 * pages]
    out_ref = rest[2 * pages]
    m_sc, l_sc, acc_sc = rest[2 * pages + 1:]
    j = pl.program_id(1)
    rows = N_HEADS * nq
    qbd = qbd_ref[0]
    thr = thr_ref[:, 0:1]
    j_sel = jsel_ref[:, 0:1]

    @pl.when(j == 0)
    def _init():
        m_sc[...] = jnp.full(m_sc.shape, NEG_BIG, F32)
        l_sc[...] = jnp.zeros(l_sc.shape, F32)
        acc_sc[...] = jnp.zeros(acc_sc.shape, F32)

    def attend(k_list, v_list, k0):
        width = LANES * len(k_list)
        kk = keys_ref[:, pl.ds(k0, width)]
        k_pos = k0 + lax.broadcasted_iota(I32, (1, width), 1)
        sel = (kk > thr) | ((kk == thr) & (k_pos <= j_sel))
        sel = jnp.concatenate([sel.astype(I32)] * N_HEADS, axis=0) > 0
        logits = jnp.concatenate([jnp.dot(qbd, kp, preferred_element_type=F32) for kp in k_list], axis=1)
        lm = jnp.where(sel, logits, NEG_BIG)
        m_old = m_sc[:, 0:1]
        m_new = jnp.maximum(m_old, jnp.max(lm, axis=1, keepdims=True))
        p = jnp.exp(lm - m_new)
        alpha = jnp.exp(m_old - m_new)
        l_sc[...] = jnp.broadcast_to(alpha * l_sc[:, 0:1] + jnp.sum(p, axis=1, keepdims=True), (rows, LANES))
        pv = jnp.zeros((rows, 512), F32)
        for i, vp in enumerate(v_list):
            pv = pv + lax.dot_general(p[:, i * LANES:(i + 1) * LANES].astype(BF16), vp, (((1,), (1,)), ((), ())),
                                      preferred_element_type=F32)
        acc_sc[...] = alpha * acc_sc[...] + pv
        m_sc[...] = jnp.broadcast_to(m_new, (rows, LANES))

    @pl.when(j < n_steps)
    def _past():
        attend([r[0].astype(BF16) for r in k_refs], [r[0].astype(BF16) for r in v_refs],
               pl.multiple_of(j * (pages * PAGE_SIZE), pages * PAGE_SIZE))

    @pl.when(j == n_steps)
    def _new():
        attend([knew_ref[0]], [vnew_ref[0]], past)
        for h in range(N_HEADS):
            hs = slice(h * HEAD_DIM, (h + 1) * HEAD_DIM)
            rs = slice(h * nq, (h + 1) * nq)
            out_ref[0, :, hs] = (acc_sc[rs, hs] / l_sc[rs, 0:1]).astype(out_ref.dtype)


def _dsa_sample_attend(page_flat, qbd, keys, thr, jsel, knew, vnew, cache_k2, cache_v2, layer, n_pool, n_pages, nq):
    Bd = qbd.shape[0]
    pages = _pick_tile(n_pages, (16, 8, 4, 2, 1))
    n_steps = n_pages // pages
    past = n_pages * PAGE_SIZE
    base = layer * n_pool
    rows = N_HEADS * nq

    def page_spec(p):
        return pl.BlockSpec((1, 512, PAGE_SIZE),
                            lambda b, j, pt: (base + pt[b * n_pages + jnp.minimum(j, n_steps - 1) * pages + p], 0, 0))

    per_b = lambda shape: pl.BlockSpec((1,) + shape, lambda b, j, pt: (b, 0, 0))
    per_q = lambda w: pl.BlockSpec((nq, w), lambda b, j, pt: (b, 0))
    grid_spec = pltpu.PrefetchScalarGridSpec(
        num_scalar_prefetch=1, grid=(Bd, n_steps + 1),
        in_specs=[per_b((rows, 512)), per_q(past + LANES), per_q(LANES), per_q(LANES),
                  per_b((512, LANES)), per_b((512, LANES))]
                 + [page_spec(p) for p in range(pages)] * 2,
        out_specs=per_b((nq, 512)),
        scratch_shapes=[pltpu.VMEM((rows, LANES), F32), pltpu.VMEM((rows, LANES), F32), pltpu.VMEM((rows, 512), F32)])
    return pl.pallas_call(
        functools.partial(_dsa_sample_attend_kernel, pages=pages, n_steps=n_steps, past=past, nq=nq),
        grid_spec=grid_spec, out_shape=jax.ShapeDtypeStruct((Bd, nq, 512), BF16),
        compiler_params=_cparams(("arbitrary", "arbitrary")), name="dsa_sample_attend",
    )(page_flat, qbd, keys, thr, jsel, knew, vnew, *([cache_k2] * pages), *([cache_v2] * pages))


def _log_sigmoid(x):
    return jnp.minimum(x, 0.0) - jnp.log1p(jnp.exp(-jnp.abs(x)))


def _mlstm_kernel(q_ref, k_ref, v_ref, misc_ref, gT_ref, c0_ref, n0_ref, m0_ref,
                  h_ref, c_out_ref, n_out_ref, m_out_ref, c_sc, n_sc, m_sc, *, lc, group):
    step = pl.program_id(1)
    hp = lax.Precision.HIGHEST

    @pl.when(step == 0)
    def _init():
        c_sc[...] = c0_ref[0]
        n_sc[...] = n0_ref[0]
        m_sc[...] = m0_ref[0]

    ti = lax.broadcasted_iota(I32, (lc, lc), 0)
    si = lax.broadcasted_iota(I32, (lc, lc), 1)
    tri = si <= ti
    for c in range(group):
        rs = slice(c * lc, (c + 1) * lc)
        for h in range(ML_HEADS):
            q = q_ref[rs, h * ML_QK_DIM:(h + 1) * ML_QK_DIM] * (ML_QK_DIM ** -0.5)
            k = k_ref[rs, h * ML_QK_DIM:(h + 1) * ML_QK_DIM]
            v = v_ref[rs, h * ML_V_DIM:(h + 1) * ML_V_DIM]
            i_col = misc_ref[rs, _M_MI + h:_M_MI + h + 1]
            lf_col = _log_sigmoid(misc_ref[rs, _M_MF + h:_M_MF + h + 1])
            i_row = gT_ref[0, h:h + 1, rs]
            lf_row = _log_sigmoid(gT_ref[0, ML_HEADS + h:ML_HEADS + h + 1, rs])
            m = m_sc[h][:, 0:1]
            b_col = jnp.sum(jnp.where(tri, lf_row, 0.0), axis=1, keepdims=True)
            b_row = jnp.sum(jnp.where(ti <= si, lf_col, 0.0), axis=0, keepdims=True)
            dmat = jnp.where(tri, b_col - b_row + i_row, -jnp.inf)
            inter = b_col + m
            mt = jnp.maximum(inter, jnp.max(dmat, axis=1, keepdims=True))
            s = lax.dot_general(q, k, (((1,), (1,)), ((), ())), preferred_element_type=F32, precision=hp)
            s = s * jnp.exp(dmat - mt)
            w = jnp.exp(inter - mt)
            num = (jnp.dot(s, v, preferred_element_type=F32, precision=hp)
                   + w * jnp.dot(q, c_sc[h], preferred_element_type=F32, precision=hp))
            den = jnp.sum(s, axis=1, keepdims=True) + w * jnp.sum(q * n_sc[h], axis=1, keepdims=True)
            h_ref[rs, h * ML_V_DIM:(h + 1) * ML_V_DIM] = num / jnp.maximum(jnp.abs(den), jnp.exp(-mt))
            bl = jnp.sum(lf_row, axis=1, keepdims=True)
            g_row = bl - b_row + i_row
            g_col = bl - b_col + i_col
            m_new = jnp.maximum(bl + m, jnp.max(g_row, axis=1, keepdims=True))
            decay = jnp.exp(bl + m - m_new)
            kw = jnp.exp(g_col - m_new) * k
            c_sc[h] = decay * c_sc[h] + lax.dot_general(kw, v, (((0,), (0,)), ((), ())),
                                                        preferred_element_type=F32, precision=hp)
            n_sc[h] = decay * n_sc[h] + jnp.sum(kw, axis=0, keepdims=True)
            m_sc[h] = jnp.broadcast_to(m_new, (1, LANES))

    @pl.when(step == pl.num_programs(1) - 1)
    def _finish():
        c_out_ref[0] = c_sc[...]
        n_out_ref[0] = n_sc[...]
        m_out_ref[0] = m_sc[...]


def _mlstm(q, k, v, misc, gT, c0, n0, m0, batch, length):
    lc = int(np.gcd(length, ML_CHUNK))
    nc = length // lc
    group = _pick_tile(nc, (4, 2, 1))
    if (group * lc) % LANES != 0 and nc != group:
        group = nc
    rows = group * lc
    steps = nc // group
    row = lambda w: pl.BlockSpec((rows, w), lambda b, s: (b * steps + s, 0))
    st = lambda shape: pl.BlockSpec((1,) + shape, lambda b, s: (b,) + (0,) * len(shape))
    n0 = n0.reshape(batch, ML_HEADS, 1, ML_QK_DIM)
    m0 = jnp.broadcast_to(m0.reshape(batch, ML_HEADS, 1, 1), (batch, ML_HEADS, 1, LANES))
    h, c, n, m = pl.pallas_call(
        functools.partial(_mlstm_kernel, lc=lc, group=group),
        grid=(batch, steps),
        in_specs=[row(256), row(256), row(512), row(LANES),
                  pl.BlockSpec((1, 2 * ML_HEADS, rows), lambda b, s: (b, 0, s)),
                  st((ML_HEADS, ML_QK_DIM, ML_V_DIM)), st((ML_HEADS, 1, ML_QK_DIM)), st((ML_HEADS, 1, LANES))],
        out_specs=[row(512), st((ML_HEADS, ML_QK_DIM, ML_V_DIM)), st((ML_HEADS, 1, ML_QK_DIM)),
                   st((ML_HEADS, 1, LANES))],
        out_shape=[jax.ShapeDtypeStruct((batch * length, 512), F32),
                   jax.ShapeDtypeStruct((batch, ML_HEADS, ML_QK_DIM, ML_V_DIM), F32),
                   jax.ShapeDtypeStruct((batch, ML_HEADS, 1, ML_QK_DIM), F32),
                   jax.ShapeDtypeStruct((batch, ML_HEADS, 1, LANES), F32)],
        scratch_shapes=[pltpu.VMEM((ML_HEADS, ML_QK_DIM, ML_V_DIM), F32), pltpu.VMEM((ML_HEADS, 1, ML_QK_DIM), F32),
                        pltpu.VMEM((ML_HEADS, 1, LANES), F32)],
        compiler_params=_cparams(("arbitrary", "arbitrary")), name="mlstm",
    )(q, k, v, misc, gT, c0, n0, m0)
    return h, c, n.reshape(batch, ML_HEADS, ML_QK_DIM), m[:, :, 0, 0]


def _merge_kernel(x_ref, a_ref, mh_ref, mo_ref, gates_ref, wb0_ref, wb1_ref, wout_ref, out_ref):
    m_out = (jax.nn.sigmoid(mo_ref[...]) * mh_ref[...]).astype(BF16)
    y0 = jnp.dot(a_ref[...], wb0_ref[...], preferred_element_type=F32)
    y1 = jnp.dot(m_out, wb1_ref[...], preferred_element_type=F32)
    merged = (jax.nn.sigmoid(gates_ref[:, :D_MODEL]) * y0 + jax.nn.sigmoid(gates_ref[:, D_MODEL:]) * y1)
    out_ref[...] = x_ref[...] + jnp.dot(merged.astype(BF16), wout_ref[...], preferred_element_type=F32)


def _merge(x, a_out, m_h, mo, gates, wb0, wb1, wout):
    R = x.shape[0]
    tm = _pick_tile(R, (512, 256, 128))
    row = lambda w: pl.BlockSpec((tm, w), lambda i: (i, 0))
    return pl.pallas_call(
        _merge_kernel, grid=(R // tm,),
        in_specs=[row(D_MODEL), row(512), row(512), row(512), row(2 * D_MODEL),
                  _const_spec(wb0.shape), _const_spec(wb1.shape), _const_spec(wout.shape)],
        out_specs=row(D_MODEL), out_shape=jax.ShapeDtypeStruct((R, D_MODEL), F32),
        compiler_params=_cparams(("parallel",)), name="merge",
    )(x, a_out, m_h, mo, gates, wb0, wb1, wout)


def _ffn_kernel(x_ref, g_ref, wgu_ref, wd_ref, out_ref, *, fc):
    x = x_ref[...]
    xn = (x * lax.rsqrt(jnp.mean(x * x, axis=-1, keepdims=True) + EPS) * g_ref[...]).astype(BF16)
    acc = x
    for f in range(0, FF_DIM, fc):
        gate = jnp.dot(xn, wgu_ref[:, f:f + fc], preferred_element_type=F32)
        up = jnp.dot(xn, wgu_ref[:, FF_DIM + f:FF_DIM + f + fc], preferred_element_type=F32)
        hidden = (gate * jax.nn.sigmoid(gate) * up).astype(BF16)
        acc = acc + jnp.dot(hidden, wd_ref[f:f + fc, :], preferred_element_type=F32)
    out_ref[...] = acc


def _ffn(x, g, wgu, wd):
    R = x.shape[0]
    tm = _pick_tile(R, (512, 256, 128))
    row = pl.BlockSpec((tm, D_MODEL), lambda i: (i, 0))
    return pl.pallas_call(
        functools.partial(_ffn_kernel, fc=256), grid=(R // tm,),
        in_specs=[row, _const_spec((1, D_MODEL)), _const_spec(wgu.shape), _const_spec(wd.shape)],
        out_specs=row, out_shape=jax.ShapeDtypeStruct((R, D_MODEL), F32),
        compiler_params=_cparams(("parallel",)), name="ffn",
    )(x, g, wgu, wd)


def _layer_weights(l, norm_mix, w_in, q_norm, k_norm, gate_bias, w_branch, w_out, norm_ffn, w_gate_up, w_down):
    w = w_in[l]
    wmain = jnp.concatenate([w[:, _C_AQ:_C_IK], w[:, _C_MQ:_C_MI], w[:, _C_MO:]], axis=1).astype(BF16)
    wmisc = jnp.concatenate([w[:, _C_IK:_C_MQ], w[:, _C_MI:_C_MO],
                             jnp.zeros((D_MODEL, LANES - 80), F32)], axis=1).astype(BF16)
    bias = jnp.concatenate([jnp.zeros((_M_MI,), F32), gate_bias[l], jnp.zeros((LANES - 80,), F32)])[None, :]
    head_id = np.arange(LANES) // HEAD_DIM
    bd = jnp.asarray((head_id[:, None] == head_id[None, :]).astype(np.float32) / HEAD_DIM)
    return dict(
        g_mix=norm_mix[l][None, :], wmain=wmain, wmisc=wmisc,
        qn=jnp.tile(q_norm[l], LANES // HEAD_DIM)[None, :], kn=jnp.tile(k_norm[l], LANES // HEAD_DIM)[None, :],
        bias=bias, bd=bd, wb0=w_branch[l, 0].astype(BF16), wb1=w_branch[l, 1].astype(BF16),
        wout=w_out[l].astype(BF16), g_ffn=norm_ffn[l][None, :], wgu=w_gate_up[l].astype(BF16),
        wd=w_down[l].astype(BF16))


def kernel(x_prompt, x_sample, cache_k, cache_v, cache_idx_k, state_C, state_n, state_m, page_table, norm_mix, w_in, q_norm, k_norm, gate_bias, w_branch, w_out, norm_ffn, w_gate_up, w_down):
    Bp, Lp, _ = x_prompt.shape
    Bd, Ld, _ = x_sample.shape
    depth = w_in.shape[0]
    n_pool = cache_k.shape[1]
    n_pages = page_table.shape[1]
    past = n_pages * PAGE_SIZE
    assert Bp == 1 and PAGE_SIZE == cache_k.shape[2]
    Rs = Bd * Ld

    xp = x_prompt.reshape(Lp, D_MODEL)
    xs = x_sample.reshape(Rs, D_MODEL)
    tab_p = _rope_tables(jnp.arange(Lp))
    tab_s = _rope_tables(jnp.tile(past + jnp.arange(Ld), Bd))
    cache_k2 = jnp.transpose(cache_k, (0, 1, 3, 4, 2)).reshape(depth * n_pool, 512, PAGE_SIZE)
    cache_v2 = jnp.transpose(cache_v, (0, 1, 3, 4, 2)).reshape(depth * n_pool, 512, PAGE_SIZE)
    cache_ik2 = jnp.transpose(cache_idx_k, (0, 1, 3, 2)).reshape(depth * n_pool, IDX_DIM, PAGE_SIZE)
    page_flat = page_table.reshape(-1).astype(I32)
    zeros_c = jnp.zeros((Bp, ML_HEADS, ML_QK_DIM, ML_V_DIM), F32)
    zeros_n = jnp.zeros((Bp, ML_HEADS, ML_QK_DIM), F32)
    zeros_m = jnp.zeros((Bp, ML_HEADS), F32)
    eye_h = jnp.eye(N_HEADS, dtype=BF16)

    outs = {k: [] for k in ("kp", "vp", "ikp", "Cp", "np", "mp", "ks", "vs", "iks", "Cs", "ns", "ms")}
    for l in range(depth):
        W = _layer_weights(l, norm_mix, w_in, q_norm, k_norm, gate_bias, w_branch, w_out, norm_ffn, w_gate_up, w_down)

        def inproj(x, tabs):
            return _inproj(x, tabs, W["g_mix"], W["wmain"], W["wmisc"], W["qn"], W["kn"], W["bias"], W["bd"])

        aq, akf, akb, avf, avb, iq, misc, ikT, mq, mk, mv, mo, gates = inproj(xp, tab_p)
        a_out = _dsa_prompt(aq, iq, misc, ikT, akb, avb)
        gT = jnp.transpose(misc[:, _M_MI:_M_MI + 2 * ML_HEADS])[None]
        m_h, C, n, m = _mlstm(mq, mk, mv, misc, gT, zeros_c, zeros_n, zeros_m, Bp, Lp)
        xp = _merge(xp, a_out, m_h, mo, gates, W["wb0"], W["wb1"], W["wout"])
        xp = _ffn(xp, W["g_ffn"], W["wgu"], W["wd"])
        outs["kp"].append(akf.reshape(Bp, Lp, N_HEADS, HEAD_DIM))
        outs["vp"].append(avf.reshape(Bp, Lp, N_HEADS, HEAD_DIM))
        outs["ikp"].append(misc[:, :IDX_DIM].reshape(Bp, Lp, IDX_DIM))
        outs["Cp"].append(C), outs["np"].append(n), outs["mp"].append(m)

        aq, akf, akb, avf, avb, iq, misc, ikT, mq, mk, mv, mo, gates = inproj(xs, tab_s)
        iqs = iq.reshape(Bd, Ld, N_IDX_HEADS, IDX_DIM).transpose(0, 2, 1, 3).reshape(Bd, N_IDX_HEADS * Ld, IDX_DIM)
        wst = misc[:, _M_IW:_M_IW + N_IDX_HEADS].reshape(Bd, Ld, N_IDX_HEADS).transpose(0, 2, 1)
        wst = jnp.broadcast_to(wst.reshape(Bd, N_IDX_HEADS * Ld, 1), (Bd, N_IDX_HEADS * Ld, LANES))
        pad_rows = lambda t: jnp.pad(t.reshape(Bd, Ld, -1).transpose(0, 2, 1), ((0, 0), (0, 0), (0, LANES - Ld)))
        iknew = pad_rows(misc[:, :IDX_DIM].astype(BF16))
        keys, thr, jsel = _dsa_sample_select(page_flat, iqs, wst, iknew, cache_ik2, l, n_pool, n_pages, Ld)
        qbd = jnp.einsum("bqhd,hg->bhqgd", aq.reshape(Bd, Ld, N_HEADS, HEAD_DIM), eye_h)
        qbd = qbd.reshape(Bd, N_HEADS * Ld, 512)
        a_out = _dsa_sample_attend(page_flat, qbd, keys, thr, jsel, pad_rows(akb), pad_rows(avb),
                                   cache_k2, cache_v2, l, n_pool, n_pages, Ld).reshape(Rs, 512)
        gT = misc[:, _M_MI:_M_MI + 2 * ML_HEADS].reshape(Bd, Ld, 2 * ML_HEADS).transpose(0, 2, 1)
        m_h, C, n, m = _mlstm(mq, mk, mv, misc, gT, state_C[l], state_n[l], state_m[l], Bd, Ld)
        xs = _merge(xs, a_out, m_h, mo, gates, W["wb0"], W["wb1"], W["wout"])
        xs = _ffn(xs, W["g_ffn"], W["wgu"], W["wd"])
        outs["ks"].append(akf.reshape(Bd, Ld, N_HEADS, HEAD_DIM))
        outs["vs"].append(avf.reshape(Bd, Ld, N_HEADS, HEAD_DIM))
        outs["iks"].append(misc[:, :IDX_DIM].reshape(Bd, Ld, IDX_DIM))
        outs["Cs"].append(C), outs["ns"].append(n), outs["ms"].append(m)

    st = lambda k: jnp.stack(outs[k])
    return (xp.reshape(Bp, Lp, D_MODEL), xs.reshape(Bd, Ld, D_MODEL),
            st("kp"), st("vp"), st("ikp"), st("Cp"), st("np"), st("mp"),
            st("ks"), st("vs"), st("iks"), st("Cs"), st("ns"), st("ms"))
```

```python
import functools

import numpy as np
import jax
import jax.numpy as jnp
from jax import lax
from jax.experimental import pallas as pl
from jax.experimental.pallas import tpu as pltpu

D_MODEL = 1024
BRANCH_WIDTH = D_MODEL // 2
HEAD_DIM = 64
N_HEADS = BRANCH_WIDTH // HEAD_DIM
N_IDX_HEADS = 8
IDX_DIM = 64
TOPK_MAX = 256
ROPE_THETA = 500000.0
ROPE_FRACTION = 4
ML_QK_DIM = 64
ML_V_DIM = 128
ML_HEADS = BRANCH_WIDTH // ML_V_DIM
ML_CHUNK = 64
FF_DIM = -(-8 * D_MODEL // (3 * 256)) * 256
PAGE_SIZE = 128
EPS = 1e-6

LANES = 128
VMEM_LIMIT = 56 * 1024 * 1024
INT_MIN = -2 ** 31
NEG_BIG = -1e30
CAND_DEPTH = 12
INSERT_ROWS = 32

_C_AQ, _C_IK, _C_IW, _C_MQ, _C_MI, _C_MO = 0, 2048, 2112, 2120, 3144, 3152
_IN_COLS = 5712
_M_IW, _M_MI, _M_MF = 64, 72, 76

F32 = jnp.float32
BF16 = jnp.bfloat16
I32 = jnp.int32


def _cparams(sem):
    return pltpu.CompilerParams(dimension_semantics=sem, vmem_limit_bytes=VMEM_LIMIT)


def _pick_tile(n, prefs):
    for t in prefs:
        if n % t == 0:
            return t
    return n


def _const_spec(shape):
    nd = len(shape)
    return pl.BlockSpec(shape, lambda *a: (0,) * nd)


def _inproj_kernel(x_ref, g_ref, wmain_ref, wmisc_ref, qn_ref, kn_ref, bias_ref, cos_ref, sin_ref, bd_ref,
                   aq_ref, akf_ref, akb_ref, avf_ref, avb_ref, iq_ref, misc_ref, ikT_ref,
                   mq_ref, mk_ref, mv_ref, mo_ref, gates_ref):
    x = x_ref[...]
    tm = x.shape[0]
    xn = (x * lax.rsqrt(jnp.mean(x * x, axis=-1, keepdims=True) + EPS) * g_ref[...]).astype(BF16)
    cosv = cos_ref[...]
    sinv = sin_ref[...]
    lane = lax.broadcasted_iota(I32, (tm, LANES), 1)
    first = (lane % HEAD_DIM) < (HEAD_DIM // ROPE_FRACTION // 2)
    shift = HEAD_DIM // ROPE_FRACTION // 2

    def rope(y, c, s):
        partner = jnp.where(first, pltpu.roll(y, LANES - shift, 1), pltpu.roll(y, shift, 1))
        return y * c + partner * s

    def headnorm(y, gain):
        ms = jnp.dot(y * y, bd_ref[...], preferred_element_type=F32, precision=lax.Precision.HIGHEST)
        return y * lax.rsqrt(ms + EPS) * gain

    def proj(c0, c1):
        return jnp.dot(xn, wmain_ref[:, c0:c1], preferred_element_type=F32)

    yq = proj(0, 512)
    yk = proj(512, 1024)
    yi = proj(1536, 2048)
    for g in range(4):
        sl = slice(g * LANES, (g + 1) * LANES)
        q = rope(headnorm(yq[:, sl], qn_ref[...]), cosv, sinv)
        aq_ref[:, sl] = (q * (HEAD_DIM ** -0.5)).astype(BF16)
        k = rope(headnorm(yk[:, sl], kn_ref[...]), cosv, sinv)
        akf_ref[:, sl] = k
        akb_ref[:, sl] = k.astype(BF16)
        iq_ref[:, sl] = (rope(yi[:, sl], cosv, sinv) * (IDX_DIM ** -0.5)).astype(BF16)
    yv = proj(1024, 1536)
    avf_ref[...] = yv
    avb_ref[...] = yv.astype(BF16)
    mq_ref[...] = proj(2048, 2304)
    mk_ref[...] = proj(2304, 2560)
    mv_ref[...] = proj(2560, 3072)
    mo_ref[...] = proj(3072, 3584)
    gates_ref[...] = proj(3584, 5632)
    ym = jnp.dot(xn, wmisc_ref[...], preferred_element_type=F32) + bias_ref[...]
    is_ik = lane < IDX_DIM
    ym = rope(ym, jnp.where(is_ik, cosv, 1.0), jnp.where(is_ik, sinv, 0.0))
    misc_ref[...] = ym
    ikT_ref[...] = jnp.transpose(ym)[:IDX_DIM, :].astype(BF16)


def _rope_tables(pos):
    half = HEAD_DIM // ROPE_FRACTION // 2
    freqs = ROPE_THETA ** (-jnp.arange(half, dtype=F32) / half)
    ang = pos.astype(F32)[:, None] * freqs[None, :]
    cos, sin = jnp.cos(ang), jnp.sin(ang)
    n = pos.shape[0]
    ones = jnp.ones((n, HEAD_DIM - 2 * half), F32)
    c = jnp.concatenate([cos, cos, ones], axis=1)
    s = jnp.concatenate([-sin, sin, 0.0 * ones], axis=1)
    return jnp.tile(c, (1, LANES // HEAD_DIM)), jnp.tile(s, (1, LANES // HEAD_DIM))


def _inproj(x, pos_tables, g, wmain, wmisc, qn, kn, bias, bd):
    R = x.shape[0]
    tm = _pick_tile(R, (256, 128))
    cos_t, sin_t = pos_tables
    row = lambda w: pl.BlockSpec((tm, w), lambda i: (i, 0))
    out_shapes = [
        jax.ShapeDtypeStruct((R, 512), BF16),
        jax.ShapeDtypeStruct((R, 512), F32),
        jax.ShapeDtypeStruct((R, 512), BF16),
        jax.ShapeDtypeStruct((R, 512), F32),
        jax.ShapeDtypeStruct((R, 512), BF16),
        jax.ShapeDtypeStruct((R, 512), BF16),
        jax.ShapeDtypeStruct((R, LANES), F32),
        jax.ShapeDtypeStruct((IDX_DIM, R), BF16),
        jax.ShapeDtypeStruct((R, 256), F32),
        jax.ShapeDtypeStruct((R, 256), F32),
        jax.ShapeDtypeStruct((R, 512), F32),
        jax.ShapeDtypeStruct((R, 512), F32),
        jax.ShapeDtypeStruct((R, 2048), F32),
    ]
    out_specs = [row(512), row(512), row(512), row(512), row(512), row(512), row(LANES),
                 pl.BlockSpec((IDX_DIM, tm), lambda i: (0, i)),
                 row(256), row(256), row(512), row(512), row(2048)]
    in_specs = [row(D_MODEL), _const_spec((1, D_MODEL)), _const_spec(wmain.shape), _const_spec(wmisc.shape),
                _const_spec((1, LANES)), _const_spec((1, LANES)), _const_spec((1, LANES)),
                row(LANES), row(LANES), _const_spec((LANES, LANES))]
    return pl.pallas_call(
        _inproj_kernel, grid=(R // tm,), in_specs=in_specs, out_specs=out_specs, out_shape=out_shapes,
        compiler_params=_cparams(("parallel",)), name="inproj",
    )(x, g, wmain, wmisc, qn, kn, bias, cos_t, sin_t, bd)


def _f32_key(x):
    bits = pltpu.bitcast(x, I32)
    return bits ^ (lax.shift_right_arithmetic(bits, 31) & 0x7FFFFFFF)


def _key_f32(k):
    return pltpu.bitcast(k ^ (lax.shift_right_arithmetic(k, 31) & 0x7FFFFFFF), F32)


def _count_rows(ref, r0, rows, n_iter, inner, preds, as_key=False):
    def body(c, accs):
        accs = list(accs)
        for u in range(inner):
            k0 = pl.multiple_of((c * inner + u) * LANES, LANES)
            kk = ref[r0:r0 + rows, pl.ds(k0, LANES)]
            if as_key:
                kk = _f32_key(kk)
            for i, pred in enumerate(preds):
                accs[i] = accs[i] + pred(kk, k0).astype(I32)
        return tuple(accs)
    zero = jnp.zeros((rows, LANES), I32)
    if isinstance(n_iter, int) and n_iter <= 16:
        accs = (zero,) * len(preds)
        for c in range(n_iter):
            accs = body(c, accs)
    else:
        accs = lax.fori_loop(0, n_iter, body, (zero,) * len(preds))
    return [jnp.sum(a, axis=1, keepdims=True) for a in accs]


def _kth_largest(ref, blocks, rows, n_iter, inner, as_key=False):
    def bit_pass(bi, t_us):
        bit = lax.shift_left(jnp.int32(1), 31 - bi)
        out = []
        for (r0, kprime), t_u in zip(blocks, t_us):
            cand = jnp.broadcast_to((t_u | bit) ^ INT_MIN, (rows, LANES))
            cnt, = _count_rows(ref, r0, rows, n_iter, inner, [lambda kk, k0, cand=cand: kk >= cand], as_key)
            out.append(jnp.where(cnt >= kprime, t_u | bit, t_u))
        return tuple(out)

    t_us = lax.fori_loop(0, 32, bit_pass, tuple(jnp.zeros((rows, 1), I32) for _ in blocks))
    return [t_u ^ INT_MIN for t_u in t_us]


def _tie_cutoff(sc_ref, r0, rows, n_iter, inner, thr, need, n_idx_bits):
    thr_b = jnp.broadcast_to(thr, (rows, LANES))
    lane = lax.broadcasted_iota(I32, (rows, LANES), 1)

    def idx_pass(bi, j):
        bit = lax.shift_left(jnp.int32(1), n_idx_bits - 1 - bi)
        cand = jnp.broadcast_to(j | bit, (rows, LANES))
        cnt, = _count_rows(sc_ref, r0, rows, n_iter, inner,
                           [lambda kk, k0: (kk == thr_b) & ((lane + k0) < cand)])
        return jnp.where(cnt < need, j | bit, j)

    return lax.fori_loop(0, n_idx_bits, idx_pass, jnp.zeros((rows, 1), I32))


def _tie_walk(sc_ref, r0, rows, n_iter, inner, thr_b, need, max_need):
    lane = lax.broadcasted_iota(I32, (rows, LANES), 1)
    far = jnp.float32(3e38)

    def step(t, j_cur):
        j_b = jnp.broadcast_to(j_cur, (rows, LANES))

        def body(c, acc):
            for u in range(inner):
                k0 = pl.multiple_of((c * inner + u) * LANES, LANES)
                idx = (lane + k0).astype(F32)
                hit = (sc_ref[r0:r0 + rows, pl.ds(k0, LANES)] == thr_b) & (idx > j_b)
                acc = jnp.minimum(acc, jnp.where(hit, idx, far))
            return acc

        acc = lax.fori_loop(0, n_iter, body, jnp.full((rows, LANES), far, F32))
        return jnp.where(t < need, jnp.min(acc, axis=1, keepdims=True), j_cur)

    return lax.fori_loop(0, max_need, step, jnp.full((rows, 1), -1.0, F32)).astype(I32)


def _lane_top_candidates(sc_ref, cand_ref, r0, rows, n_iter, inner, depth):
    sub = INSERT_ROWS if rows % INSERT_ROWS == 0 else 8

    def row_block(rb, carry):
        rs = pl.multiple_of(r0 + rb * sub, sub)

        def body(c, tops):
            tops = list(tops)
            for u in range(inner):
                k0 = pl.multiple_of((c * inner + u) * LANES, LANES)
                x = sc_ref[pl.ds(rs, sub), pl.ds(k0, LANES)]
                for i in range(depth):
                    hi = jnp.maximum(tops[i], x)
                    x = jnp.minimum(tops[i], x)
                    tops[i] = hi
            return tuple(tops)

        tops = lax.fori_loop(0, n_iter, body, (jnp.full((sub, LANES), -jnp.inf, F32),) * depth)
        for i in range(depth):
            cand_ref[pl.ds(rs, sub), i * LANES:(i + 1) * LANES] = _f32_key(tops[i])
        return carry

    lax.fori_loop(0, rows // sub, row_block, 0)


def _topk_select(sc_ref, cand_ref, thr_ref, jsel_ref, blocks, rows, n_iter, inner, n_idx_bits):
    for r0, _ in blocks:
        _lane_top_candidates(sc_ref, cand_ref, r0, rows, n_iter, inner, CAND_DEPTH)
    thrs = _kth_largest(cand_ref, blocks, rows, CAND_DEPTH, 1)
    for (r0, kprime), thr_key in zip(blocks, thrs):
        thr_b = jnp.broadcast_to(_key_f32(thr_key), (rows, LANES))
        n_gt, n_ge = _count_rows(sc_ref, r0, rows, n_iter, inner,
                                 [lambda kk, k0: kk > thr_b, lambda kk, k0: kk >= thr_b])
        thr_ref[r0:r0 + rows, :] = thr_b
        jsel_ref[r0:r0 + rows, :] = jnp.full((rows, LANES), 2 ** 31 - 1, I32)
        wrong = jnp.max(((n_gt >= kprime) | (n_ge < kprime)).astype(I32))
        excess = n_ge > kprime
        need = jnp.where(excess, kprime - n_gt, 0)
        max_need = jnp.max(need)

        @pl.when((wrong == 0) & (max_need > 0))
        def _ties_only():
            @pl.when(max_need <= n_idx_bits)
            def _walk():
                j_sel = _tie_walk(sc_ref, r0, rows, n_iter, inner, thr_b, need, max_need)
                jsel_ref[r0:r0 + rows, :] = jnp.broadcast_to(jnp.where(excess, j_sel, 2 ** 31 - 1), (rows, LANES))

            @pl.when(max_need > n_idx_bits)
            def _bisect():
                j_sel = _tie_cutoff(sc_ref, r0, rows, n_iter, inner, thr_b[:, 0:1], kprime - n_gt, n_idx_bits)
                jsel_ref[r0:r0 + rows, :] = jnp.broadcast_to(jnp.where(excess, j_sel, 2 ** 31 - 1), (rows, LANES))

        @pl.when(wrong > 0)
        def _full_search():
            t_key, = _kth_largest(sc_ref, [(r0, kprime)], rows, n_iter, inner, as_key=True)
            t = _key_f32(t_key)
            t_b = jnp.broadcast_to(t, (rows, LANES))
            gt, = _count_rows(sc_ref, r0, rows, n_iter, inner, [lambda kk, k0: kk > t_b])
            j_sel = _tie_cutoff(sc_ref, r0, rows, n_iter, inner, t, kprime - gt, n_idx_bits)
            thr_ref[r0:r0 + rows, :] = t_b
            jsel_ref[r0:r0 + rows, :] = jnp.broadcast_to(j_sel, (rows, LANES))


def _dsa_prompt_kernel(qi_ref, kj_ref, last_ref,
                       aq_ref, iq_ref, misc_ref, ikT_ref, k_ref, v_ref, out_ref,
                       sc_sc, cand_sc, thr_sc, jsel_sc, qpair_sc, bias_sc, lm_sc, alpha_sc, m_sc, acc_sc,
                       *, tq, kc, topk, seq_len):
    s_id = pl.program_id(0)
    qi = qi_ref[s_id]
    kj = kj_ref[s_id]
    n_idx_bits = int(seq_len).bit_length()
    n_pairs = N_HEADS // 2
    n_cb = kc // LANES

    @pl.when(kj == 0)
    def _select():
        q_pos = qi * tq + lax.broadcasted_iota(I32, (tq, 1), 0)
        n_kc = (qi * tq + tq + kc - 1) // kc
        w_all = misc_ref[:, _M_IW:_M_IW + N_IDX_HEADS] * (N_IDX_HEADS ** -0.5)

        def score_body(c, carry):
            k0 = pl.multiple_of(c * kc, kc)
            ikc = ikT_ref[:, pl.ds(k0, kc)]
            acc = jnp.zeros((tq, kc), F32)
            for h in range(N_IDX_HEADS):
                s = jnp.dot(iq_ref[:, h * IDX_DIM:(h + 1) * IDX_DIM], ikc, preferred_element_type=F32)
                acc = acc + jnp.maximum(s, 0.0) * w_all[:, h:h + 1]
            k_pos = k0 + lax.broadcasted_iota(I32, (1, kc), 1)
            sc_sc[:, pl.ds(k0, kc)] = jnp.where(k_pos <= q_pos, acc, -jnp.inf)
            return carry

        lax.fori_loop(0, n_kc, score_body, 0)
        kprime = jnp.minimum(topk, q_pos + 1)
        rb = min(tq, LANES)
        _topk_select(sc_sc, cand_sc, thr_sc, jsel_sc, [(r0, kprime[r0:r0 + rb]) for r0 in range(0, tq, rb)],
                     rb, n_kc, kc // LANES, n_idx_bits)
        pair_half = lax.broadcasted_iota(I32, (tq, LANES), 1) // HEAD_DIM
        for j in range(n_pairs):
            q2 = aq_ref[:, j * LANES:(j + 1) * LANES]
            for half in range(2):
                qpair_sc[j, half * tq:(half + 1) * tq, :] = jnp.where(pair_half == half, q2, jnp.zeros_like(q2))
        m_sc[...] = jnp.full(m_sc.shape, NEG_BIG, F32)
        acc_sc[...] = jnp.zeros(acc_sc.shape, F32)

    k0 = pl.multiple_of(kj * kc, kc)
    thr = thr_sc[...]
    j_sel = jsel_sc[...]
    lane = lax.broadcasted_iota(I32, (1, LANES), 1)
    for c in range(n_cb):
        kk = sc_sc[:, pl.ds(k0 + c * LANES, LANES)]
        sel = (kk > thr) | ((kk == thr) & ((k0 + c * LANES + lane) <= j_sel))
        bias_sc[:, c * LANES:(c + 1) * LANES] = jnp.where(sel, 0.0, NEG_BIG)

    for j in range(n_pairs):
        s = lax.dot_general(qpair_sc[j], k_ref[:, j * LANES:(j + 1) * LANES], (((1,), (1,)), ((), ())),
                            preferred_element_type=F32)
        for half in range(2):
            h = 2 * j + half
            lm = [s[half * tq:(half + 1) * tq, c * LANES:(c + 1) * LANES] + bias_sc[:, c * LANES:(c + 1) * LANES]
                  for c in range(n_cb)]
            mx = lm[0]
            for c in range(1, n_cb):
                mx = jnp.maximum(mx, lm[c])
            m_old = m_sc[h]
            m_new = jnp.maximum(m_old, jnp.max(mx, axis=1, keepdims=True))
            for c in range(n_cb):
                lm_sc[h, :, c * LANES:(c + 1) * LANES] = lm[c] - m_new
            alpha_sc[h] = jnp.exp(m_old - m_new)
            m_sc[h] = m_new

    ones = jnp.ones((kc, LANES), BF16)
    for j in range(n_pairs):
        p = jnp.exp(lm_sc[2 * j:2 * j + 2].reshape(2 * tq, kc)).astype(BF16)
        v_aug = jnp.concatenate([v_ref[:, j * LANES:(j + 1) * LANES], ones], axis=1)
        pv = jnp.dot(p, v_aug, preferred_element_type=F32)
        for half in range(2):
            h = 2 * j + half
            alpha = alpha_sc[h]
            acc_sc[h] = jnp.concatenate([alpha, alpha], axis=1) * acc_sc[h] + pv[half * tq:(half + 1) * tq]

    @pl.when(last_ref[s_id] == 1)
    def _finish():
        for h in range(N_HEADS):
            own = slice((h % 2) * HEAD_DIM, (h % 2 + 1) * HEAD_DIM)
            den = slice(LANES + (h % 2) * HEAD_DIM, LANES + (h % 2 + 1) * HEAD_DIM)
            out_ref[:, h * HEAD_DIM:(h + 1) * HEAD_DIM] = (acc_sc[h][:, own] / acc_sc[h][:, den]).astype(out_ref.dtype)


def _dsa_prompt(aq, iq, misc, ikT, akb, avb):
    L = aq.shape[0]
    tq = _pick_tile(L, (256, 128))
    kc = _pick_tile(L, (1024, 512, 256, 128))
    topk = min(TOPK_MAX, L // 4)
    qi, kj, last = [], [], []
    for i in range(L // tq):
        n_kc = ((i + 1) * tq + kc - 1) // kc
        for j in range(n_kc):
            qi.append(i), kj.append(j), last.append(int(j == n_kc - 1))
    qi, kj, last = (jnp.asarray(np.asarray(a, np.int32)) for a in (qi, kj, last))
    qrow = lambda w: pl.BlockSpec((tq, w), lambda s, qi, kj, last: (qi[s], 0))
    krow = pl.BlockSpec((kc, 512), lambda s, qi, kj, last: (kj[s], 0))
    grid_spec = pltpu.PrefetchScalarGridSpec(
        num_scalar_prefetch=3, grid=(int(qi.shape[0]),),
        in_specs=[qrow(512), qrow(512), qrow(LANES), pl.BlockSpec((IDX_DIM, L), lambda s, qi, kj, last: (0, 0)),
                  krow, krow],
        out_specs=qrow(512),
        scratch_shapes=[pltpu.VMEM((tq, L), F32), pltpu.VMEM((tq, CAND_DEPTH * LANES), I32),
                        pltpu.VMEM((tq, LANES), F32), pltpu.VMEM((tq, LANES), I32),
                        pltpu.VMEM((N_HEADS // 2, 2 * tq, LANES), BF16), pltpu.VMEM((tq, kc), F32),
                        pltpu.VMEM((N_HEADS, tq, kc), F32), pltpu.VMEM((N_HEADS, tq, LANES), F32),
                        pltpu.VMEM((N_HEADS, tq, LANES), F32), pltpu.VMEM((N_HEADS, tq, 2 * LANES), F32)])
    return pl.pallas_call(
        functools.partial(_dsa_prompt_kernel, tq=tq, kc=kc, topk=topk, seq_len=L),
        grid_spec=grid_spec, out_shape=jax.ShapeDtypeStruct((L, 512), BF16),
        compiler_params=_cparams(("arbitrary",)), name="dsa_prompt",
    )(qi, kj, last, aq, iq, misc, ikT, akb, avb)


def _dsa_sample_select_kernel(pt_ref, iqs_ref, wst_ref, iknew_ref, *rest, pages, n_pages, topk, past, nq, batch):
    page_refs = rest[:pages]
    keys_ref, thr_ref, jsel_ref, cand_sc = rest[pages:pages + 4]
    b = pl.program_id(0)
    j = pl.program_id(1)
    last_j = n_pages // pages - 1
    iqs = iqs_ref[0]
    w = wst_ref[0][:, 0:1] * (N_IDX_HEADS ** -0.5)
    row0 = pl.multiple_of(b * nq, nq)

    def scores(ik_t):
        s = jnp.dot(iqs, ik_t, preferred_element_type=F32)
        t = jnp.maximum(s, 0.0) * w
        acc = jnp.zeros((nq, ik_t.shape[1]), F32)
        for h in range(N_IDX_HEADS):
            acc = acc + t[h * nq:(h + 1) * nq, :]
        return acc

    k0 = pl.multiple_of(j * (pages * PAGE_SIZE), pages * PAGE_SIZE)
    ik_pages = jnp.concatenate([r[0] for r in page_refs], axis=1).astype(BF16)
    keys_ref[pl.ds(row0, nq), pl.ds(k0, pages * PAGE_SIZE)] = scores(ik_pages)

    @pl.when(j == last_j)
    def _new_tokens():
        qrow = lax.broadcasted_iota(I32, (nq, LANES), 0)
        lane = lax.broadcasted_iota(I32, (nq, LANES), 1)
        keys_ref[pl.ds(row0, nq), past:past + LANES] = jnp.where(lane <= qrow, scores(iknew_ref[0]), -jnp.inf)

    @pl.when((j == last_j) & (b == batch - 1))
    def _finish():
        rows = batch * nq
        rb = min(rows, LANES)
        qrow = lax.broadcasted_iota(I32, (rb, 1), 0) % nq
        _topk_select(keys_ref, cand_sc, thr_ref, jsel_ref,
                     [(r0, jnp.minimum(topk, past + 1 + qrow)) for r0 in range(0, rows, rb)],
                     rb, n_pages + 1, 1, int(past + LANES).bit_length())


def _dsa_sample_select(page_flat, iqs, wst, iknew, cache_ik2, layer, n_pool, n_pages, nq):
    Bd = iqs.shape[0]
    pages = _pick_tile(n_pages, (32, 16, 8, 4, 2, 1))
    past = n_pages * PAGE_SIZE
    topk = min(TOPK_MAX, (past + nq) // 4)
    base = layer * n_pool
    rows = Bd * nq

    def page_spec(p):
        return pl.BlockSpec((1, IDX_DIM, PAGE_SIZE),
                            lambda b, j, pt: (base + pt[b * n_pages + j * pages + p], 0, 0))

    per_b = lambda shape: pl.BlockSpec((1,) + shape, lambda b, j, pt: (b, 0, 0))
    whole = lambda shape: pl.BlockSpec(shape, lambda b, j, pt: (0, 0))
    grid_spec = pltpu.PrefetchScalarGridSpec(
        num_scalar_prefetch=1, grid=(Bd, n_pages // pages),
        in_specs=[per_b((N_IDX_HEADS * nq, IDX_DIM)), per_b((N_IDX_HEADS * nq, LANES)), per_b((IDX_DIM, LANES))]
                 + [page_spec(p) for p in range(pages)],
        out_specs=[whole((rows, past + LANES)), whole((rows, LANES)), whole((rows, LANES))],
        scratch_shapes=[pltpu.VMEM((rows, CAND_DEPTH * LANES), I32)])
    return pl.pallas_call(
        functools.partial(_dsa_sample_select_kernel, pages=pages, n_pages=n_pages, topk=topk, past=past, nq=nq,
                          batch=Bd),
        grid_spec=grid_spec,
        out_shape=[jax.ShapeDtypeStruct((rows, past + LANES), F32), jax.ShapeDtypeStruct((rows, LANES), F32),
                   jax.ShapeDtypeStruct((rows, LANES), I32)],
        compiler_params=_cparams(("arbitrary", "arbitrary")), name="dsa_sample_select",
    )(page_flat, iqs, wst, iknew, *([cache_ik2] * pages))


def _dsa_sample_attend_kernel(pt_ref, qbd_ref, keys_ref, thr_ref, jsel_ref, knew_ref, vnew_ref, *rest,
                              pages, n_steps, past, nq):
    k_refs = rest[:pages]
    v_refs = rest[pages:2 * pages]
    out_ref = rest[2 * pages]
    m_sc, l_sc, acc_sc = rest[2 * pages + 1:]
    j = pl.program_id(1)
    rows = N_HEADS * nq
    qbd = qbd_ref[0]
    thr = thr_ref[:, 0:1]
    j_sel = jsel_ref[:, 0:1]

    @pl.when(j == 0)
    def _init():
        m_sc[...] = jnp.full(m_sc.shape, NEG_BIG, F32)
        l_sc[...] = jnp.zeros(l_sc.shape, F32)
        acc_sc[...] = jnp.zeros(acc_sc.shape, F32)

    def attend(k_list, v_list, k0):
        width = LANES * len(k_list)
        kk = keys_ref[:, pl.ds(k0, width)]
        k_pos = k0 + lax.broadcasted_iota(I32, (1, width), 1)
        sel = (kk > thr) | ((kk == thr) & (k_pos <= j_sel))
        sel = jnp.concatenate([sel.astype(I32)] * N_HEADS, axis=0) > 0
        logits = jnp.concatenate([jnp.dot(qbd, kp, preferred_element_type=F32) for kp in k_list], axis=1)
        lm = jnp.where(sel, logits, NEG_BIG)
        m_old = m_sc[:, 0:1]
        m_new = jnp.maximum(m_old, jnp.max(lm, axis=1, keepdims=True))
        p = jnp.exp(lm - m_new)
        alpha = jnp.exp(m_old - m_new)
        l_sc[...] = jnp.broadcast_to(alpha * l_sc[:, 0:1] + jnp.sum(p, axis=1, keepdims=True), (rows, LANES))
        pv = jnp.zeros((rows, 512), F32)
        for i, vp in enumerate(v_list):
            pv = pv + lax.dot_general(p[:, i * LANES:(i + 1) * LANES].astype(BF16), vp, (((1,), (1,)), ((), ())),
                                      preferred_element_type=F32)
        acc_sc[...] = alpha * acc_sc[...] + pv
        m_sc[...] = jnp.broadcast_to(m_new, (rows, LANES))

    @pl.when(j < n_steps)
    def _past():
        attend([r[0].astype(BF16) for r in k_refs], [r[0].astype(BF16) for r in v_refs],
               pl.multiple_of(j * (pages * PAGE_SIZE), pages * PAGE_SIZE))

    @pl.when(j == n_steps)
    def _new():
        attend([knew_ref[0]], [vnew_ref[0]], past)
        for h in range(N_HEADS):
            hs = slice(h * HEAD_DIM, (h + 1) * HEAD_DIM)
            rs = slice(h * nq, (h + 1) * nq)
            out_ref[0, :, hs] = (acc_sc[rs, hs] / l_sc[rs, 0:1]).astype(out_ref.dtype)


def _dsa_sample_attend(page_flat, qbd, keys, thr, jsel, knew, vnew, cache_k2, cache_v2, layer, n_pool, n_pages, nq):
    Bd = qbd.shape[0]
    pages = _pick_tile(n_pages, (16, 8, 4, 2, 1))
    n_steps = n_pages // pages
    past = n_pages * PAGE_SIZE
    base = layer * n_pool
    rows = N_HEADS * nq

    def page_spec(p):
        return pl.BlockSpec((1, 512, PAGE_SIZE),
                            lambda b, j, pt: (base + pt[b * n_pages + jnp.minimum(j, n_steps - 1) * pages + p], 0, 0))

    per_b = lambda shape: pl.BlockSpec((1,) + shape, lambda b, j, pt: (b, 0, 0))
    per_q = lambda w: pl.BlockSpec((nq, w), lambda b, j, pt: (b, 0))
    grid_spec = pltpu.PrefetchScalarGridSpec(
        num_scalar_prefetch=1, grid=(Bd, n_steps + 1),
        in_specs=[per_b((rows, 512)), per_q(past + LANES), per_q(LANES), per_q(LANES),
                  per_b((512, LANES)), per_b((512, LANES))]
                 + [page_spec(p) for p in range(pages)] * 2,
        out_specs=per_b((nq, 512)),
        scratch_shapes=[pltpu.VMEM((rows, LANES), F32), pltpu.VMEM((rows, LANES), F32), pltpu.VMEM((rows, 512), F32)])
    return pl.pallas_call(
        functools.partial(_dsa_sample_attend_kernel, pages=pages, n_steps=n_steps, past=past, nq=nq),
        grid_spec=grid_spec, out_shape=jax.ShapeDtypeStruct((Bd, nq, 512), BF16),
        compiler_params=_cparams(("arbitrary", "arbitrary")), name="dsa_sample_attend",
    )(page_flat, qbd, keys, thr, jsel, knew, vnew, *([cache_k2] * pages), *([cache_v2] * pages))


def _log_sigmoid(x):
    return jnp.minimum(x, 0.0) - jnp.log1p(jnp.exp(-jnp.abs(x)))


def _mlstm_kernel(q_ref, k_ref, v_ref, misc_ref, gT_ref, c0_ref, n0_ref, m0_ref,
                  h_ref, c_out_ref, n_out_ref, m_out_ref, c_sc, n_sc, m_sc, *, lc, group):
    step = pl.program_id(1)
    hp = lax.Precision.HIGHEST

    @pl.when(step == 0)
    def _init():
        c_sc[...] = c0_ref[0]
        n_sc[...] = n0_ref[0]
        m_sc[...] = m0_ref[0]

    nh = ML_HEADS
    ti = lax.broadcasted_iota(I32, (lc, lc), 0)
    si = lax.broadcasted_iota(I32, (lc, lc), 1)
    tri_all = lax.broadcasted_iota(I32, (nh * lc, lc), 1) <= lax.broadcasted_iota(I32, (nh * lc, lc), 0) % lc

    def per_head_rows(vals):
        return jnp.concatenate([jnp.broadcast_to(x, (lc, x.shape[1])) for x in vals], axis=0)

    def stack(vals):
        return jnp.concatenate(vals, axis=0)

    for c in range(group):
        rs = slice(c * lc, (c + 1) * lc)
        heads = range(nh)
        q = [q_ref[rs, h * ML_QK_DIM:(h + 1) * ML_QK_DIM] * (ML_QK_DIM ** -0.5) for h in heads]
        k = [k_ref[rs, h * ML_QK_DIM:(h + 1) * ML_QK_DIM] for h in heads]
        v = [v_ref[rs, h * ML_V_DIM:(h + 1) * ML_V_DIM] for h in heads]
        i_col = stack([misc_ref[rs, _M_MI + h:_M_MI + h + 1] for h in heads])
        lf_colm = _log_sigmoid(misc_ref[rs, _M_MF:_M_MF + nh])
        i_rowm = gT_ref[0, 0:nh, rs]
        lf_rowm = _log_sigmoid(gT_ref[0, nh:2 * nh, rs])
        m = [m_sc[h][:, 0:1] for h in heads]
        m_all = per_head_rows(m)
        lf_row = per_head_rows([lf_rowm[h:h + 1, :] for h in heads])
        i_row = per_head_rows([i_rowm[h:h + 1, :] for h in heads])
        b_col = jnp.sum(jnp.where(tri_all, lf_row, 0.0), axis=1, keepdims=True)
        b_rowm = stack([jnp.sum(jnp.where(ti <= si, lf_colm[:, h:h + 1], 0.0), axis=0, keepdims=True)
                        for h in heads])
        b_row = per_head_rows([b_rowm[h:h + 1, :] for h in heads])
        dmat = jnp.where(tri_all, b_col - b_row + i_row, -jnp.inf)
        inter = b_col + m_all
        mt = jnp.maximum(inter, jnp.max(dmat, axis=1, keepdims=True))
        s = stack([lax.dot_general(q[h], k[h], (((1,), (1,)), ((), ())), preferred_element_type=F32, precision=hp)
                   for h in heads]) * jnp.exp(dmat - mt)
        w = jnp.exp(inter - mt)
        num = (stack([jnp.dot(s[h * lc:(h + 1) * lc], v[h], preferred_element_type=F32, precision=hp) for h in heads])
               + w * stack([jnp.dot(q[h], c_sc[h], preferred_element_type=F32, precision=hp) for h in heads]))
        den = (jnp.sum(s, axis=1, keepdims=True)
               + w * stack([jnp.sum(q[h] * n_sc[h], axis=1, keepdims=True) for h in heads]))
        h_out = num / jnp.maximum(jnp.abs(den), jnp.exp(-mt))
        bl = jnp.sum(lf_rowm, axis=1, keepdims=True)
        m_vec = stack(m)
        m_new = jnp.maximum(bl + m_vec, jnp.max(bl - b_rowm + i_rowm, axis=1, keepdims=True))
        decay = jnp.exp(bl + m_vec - m_new)
        bl_all = per_head_rows([bl[h:h + 1, :] for h in heads])
        m_new_all = per_head_rows([m_new[h:h + 1, :] for h in heads])
        kw = jnp.exp(bl_all - b_col + i_col - m_new_all) * stack(k)
        for h in heads:
            hr = slice(h * lc, (h + 1) * lc)
            h_ref[rs, h * ML_V_DIM:(h + 1) * ML_V_DIM] = h_out[hr]
            c_sc[h] = decay[h:h + 1, :] * c_sc[h] + lax.dot_general(kw[hr], v[h], (((0,), (0,)), ((), ())),
                                                                    preferred_element_type=F32, precision=hp)
            n_sc[h] = decay[h:h + 1, :] * n_sc[h] + jnp.sum(kw[hr], axis=0, keepdims=True)
            m_sc[h] = jnp.broadcast_to(m_new[h:h + 1, :], (1, LANES))

    @pl.when(step == pl.num_programs(1) - 1)
    def _finish():
        c_out_ref[0] = c_sc[...]
        n_out_ref[0] = n_sc[...]
        m_out_ref[0] = m_sc[...]


def _mlstm(q, k, v, misc, gT, c0, n0, m0, batch, length):
    lc = int(np.gcd(length, ML_CHUNK))
    nc = length // lc
    group = _pick_tile(nc, (4, 2, 1))
    if (group * lc) % LANES != 0 and nc != group:
        group = nc
    rows = group * lc
    steps = nc // group
    row = lambda w: pl.BlockSpec((rows, w), lambda b, s: (b * steps + s, 0))
    st = lambda shape: pl.BlockSpec((1,) + shape, lambda b, s: (b,) + (0,) * len(shape))
    n0 = n0.reshape(batch, ML_HEADS, 1, ML_QK_DIM)
    m0 = jnp.broadcast_to(m0.reshape(batch, ML_HEADS, 1, 1), (batch, ML_HEADS, 1, LANES))
    h, c, n, m = pl.pallas_call(
        functools.partial(_mlstm_kernel, lc=lc, group=group),
        grid=(batch, steps),
        in_specs=[row(256), row(256), row(512), row(LANES),
                  pl.BlockSpec((1, 2 * ML_HEADS, rows), lambda b, s: (b, 0, s)),
                  st((ML_HEADS, ML_QK_DIM, ML_V_DIM)), st((ML_HEADS, 1, ML_QK_DIM)), st((ML_HEADS, 1, LANES))],
        out_specs=[row(512), st((ML_HEADS, ML_QK_DIM, ML_V_DIM)), st((ML_HEADS, 1, ML_QK_DIM)),
                   st((ML_HEADS, 1, LANES))],
        out_shape=[jax.ShapeDtypeStruct((batch * length, 512), F32),
                   jax.ShapeDtypeStruct((batch, ML_HEADS, ML_QK_DIM, ML_V_DIM), F32),
                   jax.ShapeDtypeStruct((batch, ML_HEADS, 1, ML_QK_DIM), F32),
                   jax.ShapeDtypeStruct((batch, ML_HEADS, 1, LANES), F32)],
        scratch_shapes=[pltpu.VMEM((ML_HEADS, ML_QK_DIM, ML_V_DIM), F32), pltpu.VMEM((ML_HEADS, 1, ML_QK_DIM), F32),
                        pltpu.VMEM((ML_HEADS, 1, LANES), F32)],
        compiler_params=_cparams(("arbitrary", "arbitrary")), name="mlstm",
    )(q, k, v, misc, gT, c0, n0, m0)
    return h, c, n.reshape(batch, ML_HEADS, ML_QK_DIM), m[:, :, 0, 0]


def _merge_kernel(x_ref, a_ref, mh_ref, mo_ref, gates_ref, wb0_ref, wb1_ref, wout_ref, out_ref):
    m_out = (jax.nn.sigmoid(mo_ref[...]) * mh_ref[...]).astype(BF16)
    y0 = jnp.dot(a_ref[...], wb0_ref[...], preferred_element_type=F32)
    y1 = jnp.dot(m_out, wb1_ref[...], preferred_element_type=F32)
    merged = (jax.nn.sigmoid(gates_ref[:, :D_MODEL]) * y0 + jax.nn.sigmoid(gates_ref[:, D_MODEL:]) * y1)
    out_ref[...] = x_ref[...] + jnp.dot(merged.astype(BF16), wout_ref[...], preferred_element_type=F32)


def _merge(x, a_out, m_h, mo, gates, wb0, wb1, wout):
    R = x.shape[0]
    tm = _pick_tile(R, (512, 256, 128))
    row = lambda w: pl.BlockSpec((tm, w), lambda i: (i, 0))
    return pl.pallas_call(
        _merge_kernel, grid=(R // tm,),
        in_specs=[row(D_MODEL), row(512), row(512), row(512), row(2 * D_MODEL),
                  _const_spec(wb0.shape), _const_spec(wb1.shape), _const_spec(wout.shape)],
        out_specs=row(D_MODEL), out_shape=jax.ShapeDtypeStruct((R, D_MODEL), F32),
        compiler_params=_cparams(("parallel",)), name="merge",
    )(x, a_out, m_h, mo, gates, wb0, wb1, wout)


def _ffn_kernel(x_ref, g_ref, wgu_ref, wd_ref, out_ref, *, fc):
    x = x_ref[...]
    xn = (x * lax.rsqrt(jnp.mean(x * x, axis=-1, keepdims=True) + EPS) * g_ref[...]).astype(BF16)
    acc = x
    for f in range(0, FF_DIM, fc):
        gate = jnp.dot(xn, wgu_ref[:, f:f + fc], preferred_element_type=F32)
        up = jnp.dot(xn, wgu_ref[:, FF_DIM + f:FF_DIM + f + fc], preferred_element_type=F32)
        hidden = (gate * jax.nn.sigmoid(gate) * up).astype(BF16)
        acc = acc + jnp.dot(hidden, wd_ref[f:f + fc, :], preferred_element_type=F32)
    out_ref[...] = acc


def _ffn(x, g, wgu, wd):
    R = x.shape[0]
    tm = _pick_tile(R, (512, 256, 128))
    row = pl.BlockSpec((tm, D_MODEL), lambda i: (i, 0))
    return pl.pallas_call(
        functools.partial(_ffn_kernel, fc=256), grid=(R // tm,),
        in_specs=[row, _const_spec((1, D_MODEL)), _const_spec(wgu.shape), _const_spec(wd.shape)],
        out_specs=row, out_shape=jax.ShapeDtypeStruct((R, D_MODEL), F32),
        compiler_params=_cparams(("parallel",)), name="ffn",
    )(x, g, wgu, wd)


def _layer_weights(l, norm_mix, w_in, q_norm, k_norm, gate_bias, w_branch, w_out, norm_ffn, w_gate_up, w_down):
    w = w_in[l]
    wmain = jnp.concatenate([w[:, _C_AQ:_C_IK], w[:, _C_MQ:_C_MI], w[:, _C_MO:]], axis=1).astype(BF16)
    wmisc = jnp.concatenate([w[:, _C_IK:_C_MQ], w[:, _C_MI:_C_MO],
                             jnp.zeros((D_MODEL, LANES - 80), F32)], axis=1).astype(BF16)
    bias = jnp.concatenate([jnp.zeros((_M_MI,), F32), gate_bias[l], jnp.zeros((LANES - 80,), F32)])[None, :]
    head_id = np.arange(LANES) // HEAD_DIM
    bd = jnp.asarray((head_id[:, None] == head_id[None, :]).astype(np.float32) / HEAD_DIM)
    return dict(
        g_mix=norm_mix[l][None, :], wmain=wmain, wmisc=wmisc,
        qn=jnp.tile(q_norm[l], LANES // HEAD_DIM)[None, :], kn=jnp.tile(k_norm[l], LANES // HEAD_DIM)[None, :],
        bias=bias, bd=bd, wb0=w_branch[l, 0].astype(BF16), wb1=w_branch[l, 1].astype(BF16),
        wout=w_out[l].astype(BF16), g_ffn=norm_ffn[l][None, :], wgu=w_gate_up[l].astype(BF16),
        wd=w_down[l].astype(BF16))


def kernel(x_prompt, x_sample, cache_k, cache_v, cache_idx_k, state_C, state_n, state_m, page_table, norm_mix, w_in, q_norm, k_norm, gate_bias, w_branch, w_out, norm_ffn, w_gate_up, w_down):
    Bp, Lp, _ = x_prompt.shape
    Bd, Ld, _ = x_sample.shape
    depth = w_in.shape[0]
    n_pool = cache_k.shape[1]
    n_pages = page_table.shape[1]
    past = n_pages * PAGE_SIZE
    assert Bp == 1 and PAGE_SIZE == cache_k.shape[2]
    Rs = Bd * Ld

    xp = x_prompt.reshape(Lp, D_MODEL)
    xs = x_sample.reshape(Rs, D_MODEL)
    tab_p = _rope_tables(jnp.arange(Lp))
    tab_s = _rope_tables(jnp.tile(past + jnp.arange(Ld), Bd))
    cache_k2 = jnp.transpose(cache_k, (0, 1, 3, 4, 2)).reshape(depth * n_pool, 512, PAGE_SIZE)
    cache_v2 = jnp.transpose(cache_v, (0, 1, 3, 4, 2)).reshape(depth * n_pool, 512, PAGE_SIZE)
    cache_ik2 = jnp.transpose(cache_idx_k, (0, 1, 3, 2)).reshape(depth * n_pool, IDX_DIM, PAGE_SIZE)
    page_flat = page_table.reshape(-1).astype(I32)
    zeros_c = jnp.zeros((Bp, ML_HEADS, ML_QK_DIM, ML_V_DIM), F32)
    zeros_n = jnp.zeros((Bp, ML_HEADS, ML_QK_DIM), F32)
    zeros_m = jnp.zeros((Bp, ML_HEADS), F32)
    eye_h = jnp.eye(N_HEADS, dtype=BF16)

    outs = {k: [] for k in ("kp", "vp", "ikp", "Cp", "np", "mp", "ks", "vs", "iks", "Cs", "ns", "ms")}
    for l in range(depth):
        W = _layer_weights(l, norm_mix, w_in, q_norm, k_norm, gate_bias, w_branch, w_out, norm_ffn, w_gate_up, w_down)

        def inproj(x, tabs):
            return _inproj(x, tabs, W["g_mix"], W["wmain"], W["wmisc"], W["qn"], W["kn"], W["bias"], W["bd"])

        aq, akf, akb, avf, avb, iq, misc, ikT, mq, mk, mv, mo, gates = inproj(xp, tab_p)
        a_out = _dsa_prompt(aq, iq, misc, ikT, akb, avb)
        gT = jnp.transpose(misc[:, _M_MI:_M_MI + 2 * ML_HEADS])[None]
        m_h, C, n, m = _mlstm(mq, mk, mv, misc, gT, zeros_c, zeros_n, zeros_m, Bp, Lp)
        xp = _merge(xp, a_out, m_h, mo, gates, W["wb0"], W["wb1"], W["wout"])
        xp = _ffn(xp, W["g_ffn"], W["wgu"], W["wd"])
        outs["kp"].append(akf.reshape(Bp, Lp, N_HEADS, HEAD_DIM))
        outs["vp"].append(avf.reshape(Bp, Lp, N_HEADS, HEAD_DIM))
        outs["ikp"].append(misc[:, :IDX_DIM].reshape(Bp, Lp, IDX_DIM))
        outs["Cp"].append(C), outs["np"].append(n), outs["mp"].append(m)

        aq, akf, akb, avf, avb, iq, misc, ikT, mq, mk, mv, mo, gates = inproj(xs, tab_s)
        iqs = iq.reshape(Bd, Ld, N_IDX_HEADS, IDX_DIM).transpose(0, 2, 1, 3).reshape(Bd, N_IDX_HEADS * Ld, IDX_DIM)
        wst = misc[:, _M_IW:_M_IW + N_IDX_HEADS].reshape(Bd, Ld, N_IDX_HEADS).transpose(0, 2, 1)
        wst = jnp.broadcast_to(wst.reshape(Bd, N_IDX_HEADS * Ld, 1), (Bd, N_IDX_HEADS * Ld, LANES))
        pad_rows = lambda t: jnp.pad(t.reshape(Bd, Ld, -1).transpose(0, 2, 1), ((0, 0), (0, 0), (0, LANES - Ld)))
        iknew = pad_rows(misc[:, :IDX_DIM].astype(BF16))
        keys, thr, jsel = _dsa_sample_select(page_flat, iqs, wst, iknew, cache_ik2, l, n_pool, n_pages, Ld)
        qbd = jnp.einsum("bqhd,hg->bhqgd", aq.reshape(Bd, Ld, N_HEADS, HEAD_DIM), eye_h)
        qbd = qbd.reshape(Bd, N_HEADS * Ld, 512)
        a_out = _dsa_sample_attend(page_flat, qbd, keys, thr, jsel, pad_rows(akb), pad_rows(avb),
                                   cache_k2, cache_v2, l, n_pool, n_pages, Ld).reshape(Rs, 512)
        gT = misc[:, _M_MI:_M_MI + 2 * ML_HEADS].reshape(Bd, Ld, 2 * ML_HEADS).transpose(0, 2, 1)
        m_h, C, n, m = _mlstm(mq, mk, mv, misc, gT, state_C[l], state_n[l], state_m[l], Bd, Ld)
        xs = _merge(xs, a_out, m_h, mo, gates, W["wb0"], W["wb1"], W["wout"])
        xs = _ffn(xs, W["g_ffn"], W["wgu"], W["wd"])
        outs["ks"].append(akf.reshape(Bd, Ld, N_HEADS, HEAD_DIM))
        outs["vs"].append(avf.reshape(Bd, Ld, N_HEADS, HEAD_DIM))
        outs["iks"].append(misc[:, :IDX_DIM].reshape(Bd, Ld, IDX_DIM))
        outs["Cs"].append(C), outs["ns"].append(n), outs["ms"].append(m)

    st = lambda k: jnp.stack(outs[k])
    return (xp.reshape(Bp, Lp, D_MODEL), xs.reshape(Bd, Ld, D_MODEL),
            st("kp"), st("vp"), st("ikp"), st("Cp"), st("np"), st("mp"),
            st("ks"), st("vs"), st("iks"), st("Cs"), st("ns"), st("ms"))
```

```python
import functools

import numpy as np
import jax
import jax.numpy as jnp
from jax import lax
from jax.experimental import pallas as pl
from jax.experimental.pallas import tpu as pltpu

D_MODEL = 1024
BRANCH_WIDTH = D_MODEL // 2
HEAD_DIM = 64
N_HEADS = BRANCH_WIDTH // HEAD_DIM
N_IDX_HEADS = 8
IDX_DIM = 64
TOPK_MAX = 256
ROPE_THETA = 500000.0
ROPE_FRACTION = 4
ML_QK_DIM = 64
ML_V_DIM = 128
ML_HEADS = BRANCH_WIDTH // ML_V_DIM
ML_CHUNK = 64
FF_DIM = -(-8 * D_MODEL // (3 * 256)) * 256
PAGE_SIZE = 128
EPS = 1e-6

LANES = 128
VMEM_LIMIT = 56 * 1024 * 1024
INT_MIN = -2 ** 31
NEG_BIG = -1e30
CAND_DEPTH = 12
INSERT_ROWS = 32
BIT_PASS_UNROLL = 8

_C_AQ, _C_IK, _C_IW, _C_MQ, _C_MI, _C_MO = 0, 2048, 2112, 2120, 3144, 3152
_IN_COLS = 5712
_M_IW, _M_MI, _M_MF = 64, 72, 76

F32 = jnp.float32
BF16 = jnp.bfloat16
I32 = jnp.int32


def _cparams(sem):
    return pltpu.CompilerParams(dimension_semantics=sem, vmem_limit_bytes=VMEM_LIMIT)


def _pick_tile(n, prefs):
    for t in prefs:
        if n % t == 0:
            return t
    return n


def _const_spec(shape):
    nd = len(shape)
    return pl.BlockSpec(shape, lambda *a: (0,) * nd)


def _inproj_kernel(x_ref, g_ref, wmain_ref, wmisc_ref, qn_ref, kn_ref, bias_ref, cos_ref, sin_ref, bd_ref,
                   aq_ref, akf_ref, akb_ref, avf_ref, avb_ref, iq_ref, misc_ref, ikT_ref,
                   mq_ref, mk_ref, mv_ref, mo_ref, gates_ref):
    x = x_ref[...]
    tm = x.shape[0]
    xn = (x * lax.rsqrt(jnp.mean(x * x, axis=-1, keepdims=True) + EPS) * g_ref[...]).astype(BF16)
    cosv = cos_ref[...]
    sinv = sin_ref[...]
    lane = lax.broadcasted_iota(I32, (tm, LANES), 1)
    first = (lane % HEAD_DIM) < (HEAD_DIM // ROPE_FRACTION // 2)
    shift = HEAD_DIM // ROPE_FRACTION // 2

    def rope(y, c, s):
        partner = jnp.where(first, pltpu.roll(y, LANES - shift, 1), pltpu.roll(y, shift, 1))
        return y * c + partner * s

    def headnorm(y, gain):
        ms = jnp.dot(y * y, bd_ref[...], preferred_element_type=F32, precision=lax.Precision.HIGHEST)
        return y * lax.rsqrt(ms + EPS) * gain

    def proj(c0, c1):
        return jnp.dot(xn, wmain_ref[:, c0:c1], preferred_element_type=F32)

    yq = proj(0, 512)
    yk = proj(512, 1024)
    yi = proj(1536, 2048)
    for g in range(4):
        sl = slice(g * LANES, (g + 1) * LANES)
        q = rope(headnorm(yq[:, sl], qn_ref[...]), cosv, sinv)
        aq_ref[:, sl] = (q * (HEAD_DIM ** -0.5)).astype(BF16)
        k = rope(headnorm(yk[:, sl], kn_ref[...]), cosv, sinv)
        akf_ref[:, sl] = k
        akb_ref[:, sl] = k.astype(BF16)
        iq_ref[:, sl] = (rope(yi[:, sl], cosv, sinv) * (IDX_DIM ** -0.5)).astype(BF16)
    yv = proj(1024, 1536)
    avf_ref[...] = yv
    avb_ref[...] = yv.astype(BF16)
    mq_ref[...] = proj(2048, 2304)
    mk_ref[...] = proj(2304, 2560)
    mv_ref[...] = proj(2560, 3072)
    mo_ref[...] = proj(3072, 3584)
    gates_ref[...] = proj(3584, 5632)
    ym = jnp.dot(xn, wmisc_ref[...], preferred_element_type=F32) + bias_ref[...]
    is_ik = lane < IDX_DIM
    ym = rope(ym, jnp.where(is_ik, cosv, 1.0), jnp.where(is_ik, sinv, 0.0))
    misc_ref[...] = ym
    ikT_ref[...] = jnp.transpose(ym)[:IDX_DIM, :].astype(BF16)


def _rope_tables(pos):
    half = HEAD_DIM // ROPE_FRACTION // 2
    freqs = ROPE_THETA ** (-jnp.arange(half, dtype=F32) / half)
    ang = pos.astype(F32)[:, None] * freqs[None, :]
    cos, sin = jnp.cos(ang), jnp.sin(ang)
    n = pos.shape[0]
    ones = jnp.ones((n, HEAD_DIM - 2 * half), F32)
    c = jnp.concatenate([cos, cos, ones], axis=1)
    s = jnp.concatenate([-sin, sin, 0.0 * ones], axis=1)
    return jnp.tile(c, (1, LANES // HEAD_DIM)), jnp.tile(s, (1, LANES // HEAD_DIM))


def _inproj(x, pos_tables, g, wmain, wmisc, qn, kn, bias, bd):
    R = x.shape[0]
    tm = _pick_tile(R, (256, 128))
    cos_t, sin_t = pos_tables
    row = lambda w: pl.BlockSpec((tm, w), lambda i: (i, 0))
    out_shapes = [
        jax.ShapeDtypeStruct((R, 512), BF16),
        jax.ShapeDtypeStruct((R, 512), F32),
        jax.ShapeDtypeStruct((R, 512), BF16),
        jax.ShapeDtypeStruct((R, 512), F32),
        jax.ShapeDtypeStruct((R, 512), BF16),
        jax.ShapeDtypeStruct((R, 512), BF16),
        jax.ShapeDtypeStruct((R, LANES), F32),
        jax.ShapeDtypeStruct((IDX_DIM, R), BF16),
        jax.ShapeDtypeStruct((R, 256), F32),
        jax.ShapeDtypeStruct((R, 256), F32),
        jax.ShapeDtypeStruct((R, 512), F32),
        jax.ShapeDtypeStruct((R, 512), F32),
        jax.ShapeDtypeStruct((R, 2048), F32),
    ]
    out_specs = [row(512), row(512), row(512), row(512), row(512), row(512), row(LANES),
                 pl.BlockSpec((IDX_DIM, tm), lambda i: (0, i)),
                 row(256), row(256), row(512), row(512), row(2048)]
    in_specs = [row(D_MODEL), _const_spec((1, D_MODEL)), _const_spec(wmain.shape), _const_spec(wmisc.shape),
                _const_spec((1, LANES)), _const_spec((1, LANES)), _const_spec((1, LANES)),
                row(LANES), row(LANES), _const_spec((LANES, LANES))]
    return pl.pallas_call(
        _inproj_kernel, grid=(R // tm,), in_specs=in_specs, out_specs=out_specs, out_shape=out_shapes,
        compiler_params=_cparams(("parallel",)), name="inproj",
    )(x, g, wmain, wmisc, qn, kn, bias, cos_t, sin_t, bd)


def _f32_key(x):
    bits = pltpu.bitcast(x, I32)
    return bits ^ (lax.shift_right_arithmetic(bits, 31) & 0x7FFFFFFF)


def _key_f32(k):
    return pltpu.bitcast(k ^ (lax.shift_right_arithmetic(k, 31) & 0x7FFFFFFF), F32)


def _count_rows(ref, r0, rows, n_iter, inner, preds, as_key=False):
    def body(c, accs):
        accs = list(accs)
        for u in range(inner):
            k0 = pl.multiple_of((c * inner + u) * LANES, LANES)
            kk = ref[r0:r0 + rows, pl.ds(k0, LANES)]
            if as_key:
                kk = _f32_key(kk)
            for i, pred in enumerate(preds):
                accs[i] = accs[i] + pred(kk, k0).astype(I32)
        return tuple(accs)
    zero = jnp.zeros((rows, LANES), I32)
    if isinstance(n_iter, int) and n_iter <= 16:
        accs = (zero,) * len(preds)
        for c in range(n_iter):
            accs = body(c, accs)
    else:
        accs = lax.fori_loop(0, n_iter, body, (zero,) * len(preds))
    return [jnp.sum(a, axis=1, keepdims=True) for a in accs]


def _kth_largest(ref, blocks, rows, n_iter, inner, as_key=False):
    def bit_pass(bi, t_us):
        bit = lax.shift_left(jnp.int32(1), 31 - bi)
        out = []
        for (r0, kprime), t_u in zip(blocks, t_us):
            cand = jnp.broadcast_to((t_u | bit) ^ INT_MIN, (rows, LANES))
            cnt, = _count_rows(ref, r0, rows, n_iter, inner, [lambda kk, k0, cand=cand: kk >= cand], as_key)
            out.append(jnp.where(cnt >= kprime, t_u | bit, t_u))
        return tuple(out)

    unroll = BIT_PASS_UNROLL if isinstance(n_iter, int) else 1
    t_us = lax.fori_loop(0, 32, bit_pass, tuple(jnp.zeros((rows, 1), I32) for _ in blocks), unroll=unroll)
    return [t_u ^ INT_MIN for t_u in t_us]


def _tie_cutoff(sc_ref, r0, rows, n_iter, inner, thr, need, n_idx_bits):
    thr_b = jnp.broadcast_to(thr, (rows, LANES))
    lane = lax.broadcasted_iota(I32, (rows, LANES), 1)

    def idx_pass(bi, j):
        bit = lax.shift_left(jnp.int32(1), n_idx_bits - 1 - bi)
        cand = jnp.broadcast_to(j | bit, (rows, LANES))
        cnt, = _count_rows(sc_ref, r0, rows, n_iter, inner,
                           [lambda kk, k0: (kk == thr_b) & ((lane + k0) < cand)])
        return jnp.where(cnt < need, j | bit, j)

    return lax.fori_loop(0, n_idx_bits, idx_pass, jnp.zeros((rows, 1), I32))


def _tie_walk(sc_ref, r0, rows, n_iter, inner, thr_b, need, max_need):
    lane = lax.broadcasted_iota(I32, (rows, LANES), 1)
    far = jnp.float32(3e38)

    def step(t, j_cur):
        j_b = jnp.broadcast_to(j_cur, (rows, LANES))

        def body(c, acc):
            for u in range(inner):
                k0 = pl.multiple_of((c * inner + u) * LANES, LANES)
                idx = (lane + k0).astype(F32)
                hit = (sc_ref[r0:r0 + rows, pl.ds(k0, LANES)] == thr_b) & (idx > j_b)
                acc = jnp.minimum(acc, jnp.where(hit, idx, far))
            return acc

        acc = lax.fori_loop(0, n_iter, body, jnp.full((rows, LANES), far, F32))
        return jnp.where(t < need, jnp.min(acc, axis=1, keepdims=True), j_cur)

    return lax.fori_loop(0, max_need, step, jnp.full((rows, 1), -1.0, F32)).astype(I32)


def _lane_top_candidates(sc_ref, cand_ref, r0, rows, n_iter, inner, depth):
    sub = INSERT_ROWS if rows % INSERT_ROWS == 0 else 8

    def row_block(rb, carry):
        rs = pl.multiple_of(r0 + rb * sub, sub)

        def body(c, tops):
            tops = list(tops)
            for u in range(inner):
                k0 = pl.multiple_of((c * inner + u) * LANES, LANES)
                x = sc_ref[pl.ds(rs, sub), pl.ds(k0, LANES)]
                for i in range(depth):
                    hi = jnp.maximum(tops[i], x)
                    x = jnp.minimum(tops[i], x)
                    tops[i] = hi
            return tuple(tops)

        tops = lax.fori_loop(0, n_iter, body, (jnp.full((sub, LANES), -jnp.inf, F32),) * depth)
        for i in range(depth):
            cand_ref[pl.ds(rs, sub), i * LANES:(i + 1) * LANES] = _f32_key(tops[i])
        return carry

    lax.fori_loop(0, rows // sub, row_block, 0)


def _topk_select(sc_ref, cand_ref, thr_ref, jsel_ref, blocks, rows, n_iter, inner, n_idx_bits):
    for r0, _ in blocks:
        _lane_top_candidates(sc_ref, cand_ref, r0, rows, n_iter, inner, CAND_DEPTH)
    thrs = _kth_largest(cand_ref, blocks, rows, CAND_DEPTH, 1)
    for (r0, kprime), thr_key in zip(blocks, thrs):
        thr_b = jnp.broadcast_to(_key_f32(thr_key), (rows, LANES))
        n_gt, n_ge = _count_rows(sc_ref, r0, rows, n_iter, inner,
                                 [lambda kk, k0: kk > thr_b, lambda kk, k0: kk >= thr_b])
        thr_ref[r0:r0 + rows, :] = thr_b
        jsel_ref[r0:r0 + rows, :] = jnp.full((rows, LANES), 2 ** 31 - 1, I32)
        wrong = jnp.max(((n_gt >= kprime) | (n_ge < kprime)).astype(I32))
        excess = n_ge > kprime
        need = jnp.where(excess, kprime - n_gt, 0)
        max_need = jnp.max(need)

        @pl.when((wrong == 0) & (max_need > 0))
        def _ties_only():
            @pl.when(max_need <= n_idx_bits)
            def _walk():
                j_sel = _tie_walk(sc_ref, r0, rows, n_iter, inner, thr_b, need, max_need)
                jsel_ref[r0:r0 + rows, :] = jnp.broadcast_to(jnp.where(excess, j_sel, 2 ** 31 - 1), (rows, LANES))

            @pl.when(max_need > n_idx_bits)
            def _bisect():
                j_sel = _tie_cutoff(sc_ref, r0, rows, n_iter, inner, thr_b[:, 0:1], kprime - n_gt, n_idx_bits)
                jsel_ref[r0:r0 + rows, :] = jnp.broadcast_to(jnp.where(excess, j_sel, 2 ** 31 - 1), (rows, LANES))

        @pl.when(wrong > 0)
        def _full_search():
            t_key, = _kth_largest(sc_ref, [(r0, kprime)], rows, n_iter, inner, as_key=True)
            t = _key_f32(t_key)
            t_b = jnp.broadcast_to(t, (rows, LANES))
            gt, = _count_rows(sc_ref, r0, rows, n_iter, inner, [lambda kk, k0: kk > t_b])
            j_sel = _tie_cutoff(sc_ref, r0, rows, n_iter, inner, t, kprime - gt, n_idx_bits)
            thr_ref[r0:r0 + rows, :] = t_b
            jsel_ref[r0:r0 + rows, :] = jnp.broadcast_to(j_sel, (rows, LANES))


def _dsa_prompt_kernel(qi_ref, kj_ref, last_ref,
                       aq_ref, iq_ref, misc_ref, ikT_ref, k_ref, v_ref, out_ref,
                       sc_sc, cand_sc, thr_sc, jsel_sc, qpair_sc, bias_sc, lm_sc, alpha_sc, m_sc, acc_sc,
                       *, tq, kc, topk, seq_len):
    s_id = pl.program_id(0)
    qi = qi_ref[s_id]
    kj = kj_ref[s_id]
    n_idx_bits = int(seq_len).bit_length()
    n_pairs = N_HEADS // 2
    n_cb = kc // LANES

    @pl.when(kj == 0)
    def _select():
        q_pos = qi * tq + lax.broadcasted_iota(I32, (tq, 1), 0)
        n_kc = (qi * tq + tq + kc - 1) // kc
        w_all = misc_ref[:, _M_IW:_M_IW + N_IDX_HEADS] * (N_IDX_HEADS ** -0.5)

        def score_body(c, carry):
            k0 = pl.multiple_of(c * kc, kc)
            ikc = ikT_ref[:, pl.ds(k0, kc)]
            acc = jnp.zeros((tq, kc), F32)
            for h in range(N_IDX_HEADS):
                s = jnp.dot(iq_ref[:, h * IDX_DIM:(h + 1) * IDX_DIM], ikc, preferred_element_type=F32)
                acc = acc + jnp.maximum(s, 0.0) * w_all[:, h:h + 1]
            k_pos = k0 + lax.broadcasted_iota(I32, (1, kc), 1)
            sc_sc[:, pl.ds(k0, kc)] = jnp.where(k_pos <= q_pos, acc, -jnp.inf)
            return carry

        lax.fori_loop(0, n_kc, score_body, 0)
        kprime = jnp.minimum(topk, q_pos + 1)
        rb = min(tq, LANES)
        _topk_select(sc_sc, cand_sc, thr_sc, jsel_sc, [(r0, kprime[r0:r0 + rb]) for r0 in range(0, tq, rb)],
                     rb, n_kc, kc // LANES, n_idx_bits)
        pair_half = lax.broadcasted_iota(I32, (tq, LANES), 1) // HEAD_DIM
        for j in range(n_pairs):
            q2 = aq_ref[:, j * LANES:(j + 1) * LANES]
            for half in range(2):
                qpair_sc[j, half * tq:(half + 1) * tq, :] = jnp.where(pair_half == half, q2, jnp.zeros_like(q2))
        m_sc[...] = jnp.full(m_sc.shape, NEG_BIG, F32)
        acc_sc[...] = jnp.zeros(acc_sc.shape, F32)

    k0 = pl.multiple_of(kj * kc, kc)
    thr = thr_sc[...]
    j_sel = jsel_sc[...]
    lane = lax.broadcasted_iota(I32, (1, LANES), 1)
    for c in range(n_cb):
        kk = sc_sc[:, pl.ds(k0 + c * LANES, LANES)]
        sel = (kk > thr) | ((kk == thr) & ((k0 + c * LANES + lane) <= j_sel))
        bias_sc[:, c * LANES:(c + 1) * LANES] = jnp.where(sel, 0.0, NEG_BIG)

    for j in range(n_pairs):
        s = lax.dot_general(qpair_sc[j], k_ref[:, j * LANES:(j + 1) * LANES], (((1,), (1,)), ((), ())),
                            preferred_element_type=F32)
        for half in range(2):
            h = 2 * j + half
            lm = [s[half * tq:(half + 1) * tq, c * LANES:(c + 1) * LANES] + bias_sc[:, c * LANES:(c + 1) * LANES]
                  for c in range(n_cb)]
            mx = lm[0]
            for c in range(1, n_cb):
                mx = jnp.maximum(mx, lm[c])
            m_old = m_sc[h]
            m_new = jnp.maximum(m_old, jnp.max(mx, axis=1, keepdims=True))
            for c in range(n_cb):
                lm_sc[h, :, c * LANES:(c + 1) * LANES] = lm[c] - m_new
            alpha_sc[h] = jnp.exp(m_old - m_new)
            m_sc[h] = m_new

    ones = jnp.ones((kc, LANES), BF16)
    for j in range(n_pairs):
        p = jnp.exp(lm_sc[2 * j:2 * j + 2].reshape(2 * tq, kc)).astype(BF16)
        v_aug = jnp.concatenate([v_ref[:, j * LANES:(j + 1) * LANES], ones], axis=1)
        pv = jnp.dot(p, v_aug, preferred_element_type=F32)
        for half in range(2):
            h = 2 * j + half
            alpha = alpha_sc[h]
            acc_sc[h] = jnp.concatenate([alpha, alpha], axis=1) * acc_sc[h] + pv[half * tq:(half + 1) * tq]

    @pl.when(last_ref[s_id] == 1)
    def _finish():
        for h in range(N_HEADS):
            own = slice((h % 2) * HEAD_DIM, (h % 2 + 1) * HEAD_DIM)
            den = slice(LANES + (h % 2) * HEAD_DIM, LANES + (h % 2 + 1) * HEAD_DIM)
            out_ref[:, h * HEAD_DIM:(h + 1) * HEAD_DIM] = (acc_sc[h][:, own] / acc_sc[h][:, den]).astype(out_ref.dtype)


def _dsa_prompt(aq, iq, misc, ikT, akb, avb):
    L = aq.shape[0]
    tq = _pick_tile(L, (256, 128))
    kc = _pick_tile(L, (1024, 512, 256, 128))
    topk = min(TOPK_MAX, L // 4)
    qi, kj, last = [], [], []
    for i in range(L // tq):
        n_kc = ((i + 1) * tq + kc - 1) // kc
        for j in range(n_kc):
            qi.append(i), kj.append(j), last.append(int(j == n_kc - 1))
    qi, kj, last = (jnp.asarray(np.asarray(a, np.int32)) for a in (qi, kj, last))
    qrow = lambda w: pl.BlockSpec((tq, w), lambda s, qi, kj, last: (qi[s], 0))
    krow = pl.BlockSpec((kc, 512), lambda s, qi, kj, last: (kj[s], 0))
    grid_spec = pltpu.PrefetchScalarGridSpec(
        num_scalar_prefetch=3, grid=(int(qi.shape[0]),),
        in_specs=[qrow(512), qrow(512), qrow(LANES), pl.BlockSpec((IDX_DIM, L), lambda s, qi, kj, last: (0, 0)),
                  krow, krow],
        out_specs=qrow(512),
        scratch_shapes=[pltpu.VMEM((tq, L), F32), pltpu.VMEM((tq, CAND_DEPTH * LANES), I32),
                        pltpu.VMEM((tq, LANES), F32), pltpu.VMEM((tq, LANES), I32),
                        pltpu.VMEM((N_HEADS // 2, 2 * tq, LANES), BF16), pltpu.VMEM((tq, kc), F32),
                        pltpu.VMEM((N_HEADS, tq, kc), F32), pltpu.VMEM((N_HEADS, tq, LANES), F32),
                        pltpu.VMEM((N_HEADS, tq, LANES), F32), pltpu.VMEM((N_HEADS, tq, 2 * LANES), F32)])
    return pl.pallas_call(
        functools.partial(_dsa_prompt_kernel, tq=tq, kc=kc, topk=topk, seq_len=L),
        grid_spec=grid_spec, out_shape=jax.ShapeDtypeStruct((L, 512), BF16),
        compiler_params=_cparams(("arbitrary",)), name="dsa_prompt",
    )(qi, kj, last, aq, iq, misc, ikT, akb, avb)


def _dsa_sample_select_kernel(pt_ref, iqs_ref, wst_ref, iknew_ref, *rest, pages, n_pages, topk, past, nq, batch):
    page_refs = rest[:pages]
    keys_ref, thr_ref, jsel_ref, cand_sc = rest[pages:pages + 4]
    b = pl.program_id(0)
    j = pl.program_id(1)
    last_j = n_pages // pages - 1
    iqs = iqs_ref[0]
    w = wst_ref[0][:, 0:1] * (N_IDX_HEADS ** -0.5)
    row0 = pl.multiple_of(b * nq, nq)

    def scores(ik_t):
        s = jnp.dot(iqs, ik_t, preferred_element_type=F32)
        t = jnp.maximum(s, 0.0) * w
        acc = jnp.zeros((nq, ik_t.shape[1]), F32)
        for h in range(N_IDX_HEADS):
            acc = acc + t[h * nq:(h + 1) * nq, :]
        return acc

    k0 = pl.multiple_of(j * (pages * PAGE_SIZE), pages * PAGE_SIZE)
    ik_pages = jnp.concatenate([r[0] for r in page_refs], axis=1).astype(BF16)
    keys_ref[pl.ds(row0, nq), pl.ds(k0, pages * PAGE_SIZE)] = scores(ik_pages)

    @pl.when(j == last_j)
    def _new_tokens():
        qrow = lax.broadcasted_iota(I32, (nq, LANES), 0)
        lane = lax.broadcasted_iota(I32, (nq, LANES), 1)
        keys_ref[pl.ds(row0, nq), past:past + LANES] = jnp.where(lane <= qrow, scores(iknew_ref[0]), -jnp.inf)

    @pl.when((j == last_j) & (b == batch - 1))
    def _finish():
        rows = batch * nq
        rb = min(rows, LANES)
        qrow = lax.broadcasted_iota(I32, (rb, 1), 0) % nq
        _topk_select(keys_ref, cand_sc, thr_ref, jsel_ref,
                     [(r0, jnp.minimum(topk, past + 1 + qrow)) for r0 in range(0, rows, rb)],
                     rb, n_pages + 1, 1, int(past + LANES).bit_length())


def _dsa_sample_select(page_flat, iqs, wst, iknew, cache_ik2, layer, n_pool, n_pages, nq):
    Bd = iqs.shape[0]
    pages = _pick_tile(n_pages, (32, 16, 8, 4, 2, 1))
    past = n_pages * PAGE_SIZE
    topk = min(TOPK_MAX, (past + nq) // 4)
    base = layer * n_pool
    rows = Bd * nq

    def page_spec(p):
        return pl.BlockSpec((1, IDX_DIM, PAGE_SIZE),
                            lambda b, j, pt: (base + pt[b * n_pages + j * pages + p], 0, 0))

    per_b = lambda shape: pl.BlockSpec((1,) + shape, lambda b, j, pt: (b, 0, 0))
    whole = lambda shape: pl.BlockSpec(shape, lambda b, j, pt: (0, 0))
    grid_spec = pltpu.PrefetchScalarGridSpec(
        num_scalar_prefetch=1, grid=(Bd, n_pages // pages),
        in_specs=[per_b((N_IDX_HEADS * nq, IDX_DIM)), per_b((N_IDX_HEADS * nq, LANES)), per_b((IDX_DIM, LANES))]
                 + [page_spec(p) for p in range(pages)],
        out_specs=[whole((rows, past + LANES)), whole((rows, LANES)), whole((rows, LANES))],
        scratch_shapes=[pltpu.VMEM((rows, CAND_DEPTH * LANES), I32)])
    return pl.pallas_call(
        functools.partial(_dsa_sample_select_kernel, pages=pages, n_pages=n_pages, topk=topk, past=past, nq=nq,
                          batch=Bd),
        grid_spec=grid_spec,
        out_shape=[jax.ShapeDtypeStruct((rows, past + LANES), F32), jax.ShapeDtypeStruct((rows, LANES), F32),
                   jax.ShapeDtypeStruct((rows, LANES), I32)],
        compiler_params=_cparams(("arbitrary", "arbitrary")), name="dsa_sample_select",
    )(page_flat, iqs, wst, iknew, *([cache_ik2] * pages))


def _dsa_sample_attend_kernel(pt_ref, qbd_ref, keys_ref, thr_ref, jsel_ref, knew_ref, vnew_ref, *rest,
                              pages, n_steps, past, nq):
    k_refs = rest[:pages]
    v_refs = rest[pages:2 * pages]
    out_ref = rest[2 * pages]
    m_sc, l_sc, acc_sc = rest[2 * pages + 1:]
    j = pl.program_id(1)
    rows = N_HEADS * nq
    qbd = qbd_ref[0]
    thr = thr_ref[:, 0:1]
    j_sel = jsel_ref[:, 0:1]

    @pl.when(j == 0)
    def _init():
        m_sc[...] = jnp.full(m_sc.shape, NEG_BIG, F32)
        l_sc[...] = jnp.zeros(l_sc.shape, F32)
        acc_sc[...] = jnp.zeros(acc_sc.shape, F32)

    def attend(k_list, v_list, k0):
        width = LANES * len(k_list)
        kk = keys_ref[:, pl.ds(k0, width)]
        k_pos = k0 + lax.broadcasted_iota(I32, (1, width), 1)
        sel = (kk > thr) | ((kk == thr) & (k_pos <= j_sel))
        sel = jnp.concatenate([sel.astype(I32)] * N_HEADS, axis=0) > 0
        logits = jnp.concatenate([jnp.dot(qbd, kp, preferred_element_type=F32) for kp in k_list], axis=1)
        lm = jnp.where(sel, logits, NEG_BIG)
        m_old = m_sc[:, 0:1]
        m_new = jnp.maximum(m_old, jnp.max(lm, axis=1, keepdims=True))
        p = jnp.exp(lm - m_new)
        alpha = jnp.exp(m_old - m_new)
        l_sc[...] = jnp.broadcast_to(alpha * l_sc[:, 0:1] + jnp.sum(p, axis=1, keepdims=True), (rows, LANES))
        pv = jnp.zeros((rows, 512), F32)
        for i, vp in enumerate(v_list):
            pv = pv + lax.dot_general(p[:, i * LANES:(i + 1) * LANES].astype(BF16), vp, (((1,), (1,)), ((), ())),
                                      preferred_element_type=F32)
        acc_sc[...] = alpha * acc_sc[...] + pv
        m_sc[...] = jnp.broadcast_to(m_new, (rows, LANES))

    @pl.when(j < n_steps)
    def _past():
        attend([r[0].astype(BF16) for r in k_refs], [r[0].astype(BF16) for r in v_refs],
               pl.multiple_of(j * (pages * PAGE_SIZE), pages * PAGE_SIZE))

    @pl.when(j == n_steps)
    def _new():
        attend([knew_ref[0]], [vnew_ref[0]], past)
        for h in range(N_HEADS):
            hs = slice(h * HEAD_DIM, (h + 1) * HEAD_DIM)
            rs = slice(h * nq, (h + 1) * nq)
            out_ref[0, :, hs] = (acc_sc[rs, hs] / l_sc[rs, 0:1]).astype(out_ref.dtype)


def _dsa_sample_attend(page_flat, qbd, keys, thr, jsel, knew, vnew, cache_k2, cache_v2, layer, n_pool, n_pages, nq):
    Bd = qbd.shape[0]
    pages = _pick_tile(n_pages, (16, 8, 4, 2, 1))
    n_steps = n_pages // pages
    past = n_pages * PAGE_SIZE
    base = layer * n_pool
    rows = N_HEADS * nq

    def page_spec(p):
        return pl.BlockSpec((1, 512, PAGE_SIZE),
                            lambda b, j, pt: (base + pt[b * n_pages + jnp.minimum(j, n_steps - 1) * pages + p], 0, 0))

    per_b = lambda shape: pl.BlockSpec((1,) + shape, lambda b, j, pt: (b, 0, 0))
    per_q = lambda w: pl.BlockSpec((nq, w), lambda b, j, pt: (b, 0))
    grid_spec = pltpu.PrefetchScalarGridSpec(
        num_scalar_prefetch=1, grid=(Bd, n_steps + 1),
        in_specs=[per_b((rows, 512)), per_q(past + LANES), per_q(LANES), per_q(LANES),
                  per_b((512, LANES)), per_b((512, LANES))]
                 + [page_spec(p) for p in range(pages)] * 2,
        out_specs=per_b((nq, 512)),
        scratch_shapes=[pltpu.VMEM((rows, LANES), F32), pltpu.VMEM((rows, LANES), F32), pltpu.VMEM((rows, 512), F32)])
    return pl.pallas_call(
        functools.partial(_dsa_sample_attend_kernel, pages=pages, n_steps=n_steps, past=past, nq=nq),
        grid_spec=grid_spec, out_shape=jax.ShapeDtypeStruct((Bd, nq, 512), BF16),
        compiler_params=_cparams(("arbitrary", "arbitrary")), name="dsa_sample_attend",
    )(page_flat, qbd, keys, thr, jsel, knew, vnew, *([cache_k2] * pages), *([cache_v2] * pages))


def _log_sigmoid(x):
    return jnp.minimum(x, 0.0) - jnp.log1p(jnp.exp(-jnp.abs(x)))


def _mlstm_kernel(q_ref, k_ref, v_ref, misc_ref, gT_ref, c0_ref, n0_ref, m0_ref,
                  h_ref, c_out_ref, n_out_ref, m_out_ref, c_sc, n_sc, m_sc, *, lc, group):
    step = pl.program_id(1)
    hp = lax.Precision.HIGHEST

    @pl.when(step == 0)
    def _init():
        c_sc[...] = c0_ref[0]
        n_sc[...] = n0_ref[0]
        m_sc[...] = m0_ref[0]

    nh = ML_HEADS
    ti = lax.broadcasted_iota(I32, (lc, lc), 0)
    si = lax.broadcasted_iota(I32, (lc, lc), 1)
    tri_all = lax.broadcasted_iota(I32, (nh * lc, lc), 1) <= lax.broadcasted_iota(I32, (nh * lc, lc), 0) % lc

    def per_head_rows(vals):
        return jnp.concatenate([jnp.broadcast_to(x, (lc, x.shape[1])) for x in vals], axis=0)

    def stack(vals):
        return jnp.concatenate(vals, axis=0)

    for c in range(group):
        rs = slice(c * lc, (c + 1) * lc)
        heads = range(nh)
        q = [q_ref[rs, h * ML_QK_DIM:(h + 1) * ML_QK_DIM] * (ML_QK_DIM ** -0.5) for h in heads]
        k = [k_ref[rs, h * ML_QK_DIM:(h + 1) * ML_QK_DIM] for h in heads]
        v = [v_ref[rs, h * ML_V_DIM:(h + 1) * ML_V_DIM] for h in heads]
        i_col = stack([misc_ref[rs, _M_MI + h:_M_MI + h + 1] for h in heads])
        lf_colm = _log_sigmoid(misc_ref[rs, _M_MF:_M_MF + nh])
        i_rowm = gT_ref[0, 0:nh, rs]
        lf_rowm = _log_sigmoid(gT_ref[0, nh:2 * nh, rs])
        m = [m_sc[h][:, 0:1] for h in heads]
        m_all = per_head_rows(m)
        lf_row = per_head_rows([lf_rowm[h:h + 1, :] for h in heads])
        i_row = per_head_rows([i_rowm[h:h + 1, :] for h in heads])
        b_col = jnp.sum(jnp.where(tri_all, lf_row, 0.0), axis=1, keepdims=True)
        b_rowm = stack([jnp.sum(jnp.where(ti <= si, lf_colm[:, h:h + 1], 0.0), axis=0, keepdims=True)
                        for h in heads])
        b_row = per_head_rows([b_rowm[h:h + 1, :] for h in heads])
        dmat = jnp.where(tri_all, b_col - b_row + i_row, -jnp.inf)
        inter = b_col + m_all
        mt = jnp.maximum(inter, jnp.max(dmat, axis=1, keepdims=True))
        s = stack([lax.dot_general(q[h], k[h], (((1,), (1,)), ((), ())), preferred_element_type=F32, precision=hp)
                   for h in heads]) * jnp.exp(dmat - mt)
        w = jnp.exp(inter - mt)
        num = (stack([jnp.dot(s[h * lc:(h + 1) * lc], v[h], preferred_element_type=F32, precision=hp) for h in heads])
               + w * stack([jnp.dot(q[h], c_sc[h], preferred_element_type=F32, precision=hp) for h in heads]))
        den = (jnp.sum(s, axis=1, keepdims=True)
               + w * stack([jnp.sum(q[h] * n_sc[h], axis=1, keepdims=True) for h in heads]))
        h_out = num / jnp.maximum(jnp.abs(den), jnp.exp(-mt))
        bl = jnp.sum(lf_rowm, axis=1, keepdims=True)
        m_vec = stack(m)
        m_new = jnp.maximum(bl + m_vec, jnp.max(bl - b_rowm + i_rowm, axis=1, keepdims=True))
        decay = jnp.exp(bl + m_vec - m_new)
        bl_all = per_head_rows([bl[h:h + 1, :] for h in heads])
        m_new_all = per_head_rows([m_new[h:h + 1, :] for h in heads])
        kw = jnp.exp(bl_all - b_col + i_col - m_new_all) * stack(k)
        for h in heads:
            hr = slice(h * lc, (h + 1) * lc)
            h_ref[rs, h * ML_V_DIM:(h + 1) * ML_V_DIM] = h_out[hr]
            c_sc[h] = decay[h:h + 1, :] * c_sc[h] + lax.dot_general(kw[hr], v[h], (((0,), (0,)), ((), ())),
                                                                    preferred_element_type=F32, precision=hp)
            n_sc[h] = decay[h:h + 1, :] * n_sc[h] + jnp.sum(kw[hr], axis=0, keepdims=True)
            m_sc[h] = jnp.broadcast_to(m_new[h:h + 1, :], (1, LANES))

    @pl.when(step == pl.num_programs(1) - 1)
    def _finish():
        c_out_ref[0] = c_sc[...]
        n_out_ref[0] = n_sc[...]
        m_out_ref[0] = m_sc[...]


def _mlstm(q, k, v, misc, gT, c0, n0, m0, batch, length):
    lc = int(np.gcd(length, ML_CHUNK))
    nc = length // lc
    group = _pick_tile(nc, (4, 2, 1))
    if (group * lc) % LANES != 0 and nc != group:
        group = nc
    rows = group * lc
    steps = nc // group
    row = lambda w: pl.BlockSpec((rows, w), lambda b, s: (b * steps + s, 0))
    st = lambda shape: pl.BlockSpec((1,) + shape, lambda b, s: (b,) + (0,) * len(shape))
    n0 = n0.reshape(batch, ML_HEADS, 1, ML_QK_DIM)
    m0 = jnp.broadcast_to(m0.reshape(batch, ML_HEADS, 1, 1), (batch, ML_HEADS, 1, LANES))
    h, c, n, m = pl.pallas_call(
        functools.partial(_mlstm_kernel, lc=lc, group=group),
        grid=(batch, steps),
        in_specs=[row(256), row(256), row(512), row(LANES),
                  pl.BlockSpec((1, 2 * ML_HEADS, rows), lambda b, s: (b, 0, s)),
                  st((ML_HEADS, ML_QK_DIM, ML_V_DIM)), st((ML_HEADS, 1, ML_QK_DIM)), st((ML_HEADS, 1, LANES))],
        out_specs=[row(512), st((ML_HEADS, ML_QK_DIM, ML_V_DIM)), st((ML_HEADS, 1, ML_QK_DIM)),
                   st((ML_HEADS, 1, LANES))],
        out_shape=[jax.ShapeDtypeStruct((batch * length, 512), F32),
                   jax.ShapeDtypeStruct((batch, ML_HEADS, ML_QK_DIM, ML_V_DIM), F32),
                   jax.ShapeDtypeStruct((batch, ML_HEADS, 1, ML_QK_DIM), F32),
                   jax.ShapeDtypeStruct((batch, ML_HEADS, 1, LANES), F32)],
        scratch_shapes=[pltpu.VMEM((ML_HEADS, ML_QK_DIM, ML_V_DIM), F32), pltpu.VMEM((ML_HEADS, 1, ML_QK_DIM), F32),
                        pltpu.VMEM((ML_HEADS, 1, LANES), F32)],
        compiler_params=_cparams(("arbitrary", "arbitrary")), name="mlstm",
    )(q, k, v, misc, gT, c0, n0, m0)
    return h, c, n.reshape(batch, ML_HEADS, ML_QK_DIM), m[:, :, 0, 0]


def _merge_kernel(x_ref, a_ref, mh_ref, mo_ref, gates_ref, wb0_ref, wb1_ref, wout_ref, out_ref):
    m_out = (jax.nn.sigmoid(mo_ref[...]) * mh_ref[...]).astype(BF16)
    y0 = jnp.dot(a_ref[...], wb0_ref[...], preferred_element_type=F32)
    y1 = jnp.dot(m_out, wb1_ref[...], preferred_element_type=F32)
    merged = (jax.nn.sigmoid(gates_ref[:, :D_MODEL]) * y0 + jax.nn.sigmoid(gates_ref[:, D_MODEL:]) * y1)
    out_ref[...] = x_ref[...] + jnp.dot(merged.astype(BF16), wout_ref[...], preferred_element_type=F32)


def _merge(x, a_out, m_h, mo, gates, wb0, wb1, wout):
    R = x.shape[0]
    tm = _pick_tile(R, (512, 256, 128))
    row = lambda w: pl.BlockSpec((tm, w), lambda i: (i, 0))
    return pl.pallas_call(
        _merge_kernel, grid=(R // tm,),
        in_specs=[row(D_MODEL), row(512), row(512), row(512), row(2 * D_MODEL),
                  _const_spec(wb0.shape), _const_spec(wb1.shape), _const_spec(wout.shape)],
        out_specs=row(D_MODEL), out_shape=jax.ShapeDtypeStruct((R, D_MODEL), F32),
        compiler_params=_cparams(("parallel",)), name="merge",
    )(x, a_out, m_h, mo, gates, wb0, wb1, wout)


def _ffn_kernel(x_ref, g_ref, wgu_ref, wd_ref, out_ref, *, fc):
    x = x_ref[...]
    xn = (x * lax.rsqrt(jnp.mean(x * x, axis=-1, keepdims=True) + EPS) * g_ref[...]).astype(BF16)
    acc = x
    for f in range(0, FF_DIM, fc):
        gate = jnp.dot(xn, wgu_ref[:, f:f + fc], preferred_element_type=F32)
        up = jnp.dot(xn, wgu_ref[:, FF_DIM + f:FF_DIM + f + fc], preferred_element_type=F32)
        hidden = (gate * jax.nn.sigmoid(gate) * up).astype(BF16)
        acc = acc + jnp.dot(hidden, wd_ref[f:f + fc, :], preferred_element_type=F32)
    out_ref[...] = acc


def _ffn(x, g, wgu, wd):
    R = x.shape[0]
    tm = _pick_tile(R, (512, 256, 128))
    row = pl.BlockSpec((tm, D_MODEL), lambda i: (i, 0))
    return pl.pallas_call(
        functools.partial(_ffn_kernel, fc=256), grid=(R // tm,),
        in_specs=[row, _const_spec((1, D_MODEL)), _const_spec(wgu.shape), _const_spec(wd.shape)],
        out_specs=row, out_shape=jax.ShapeDtypeStruct((R, D_MODEL), F32),
        compiler_params=_cparams(("parallel",)), name="ffn",
    )(x, g, wgu, wd)


def _layer_weights(l, norm_mix, w_in, q_norm, k_norm, gate_bias, w_branch, w_out, norm_ffn, w_gate_up, w_down):
    w = w_in[l]
    wmain = jnp.concatenate([w[:, _C_AQ:_C_IK], w[:, _C_MQ:_C_MI], w[:, _C_MO:]], axis=1).astype(BF16)
    wmisc = jnp.concatenate([w[:, _C_IK:_C_MQ], w[:, _C_MI:_C_MO],
                             jnp.zeros((D_MODEL, LANES - 80), F32)], axis=1).astype(BF16)
    bias = jnp.concatenate([jnp.zeros((_M_MI,), F32), gate_bias[l], jnp.zeros((LANES - 80,), F32)])[None, :]
    head_id = np.arange(LANES) // HEAD_DIM
    bd = jnp.asarray((head_id[:, None] == head_id[None, :]).astype(np.float32) / HEAD_DIM)
    return dict(
        g_mix=norm_mix[l][None, :], wmain=wmain, wmisc=wmisc,
        qn=jnp.tile(q_norm[l], LANES // HEAD_DIM)[None, :], kn=jnp.tile(k_norm[l], LANES // HEAD_DIM)[None, :],
        bias=bias, bd=bd, wb0=w_branch[l, 0].astype(BF16), wb1=w_branch[l, 1].astype(BF16),
        wout=w_out[l].astype(BF16), g_ffn=norm_ffn[l][None, :], wgu=w_gate_up[l].astype(BF16),
        wd=w_down[l].astype(BF16))


def kernel(x_prompt, x_sample, cache_k, cache_v, cache_idx_k, state_C, state_n, state_m, page_table, norm_mix, w_in, q_norm, k_norm, gate_bias, w_branch, w_out, norm_ffn, w_gate_up, w_down):
    Bp, Lp, _ = x_prompt.shape
    Bd, Ld, _ = x_sample.shape
    depth = w_in.shape[0]
    n_pool = cache_k.shape[1]
    n_pages = page_table.shape[1]
    past = n_pages * PAGE_SIZE
    assert Bp == 1 and PAGE_SIZE == cache_k.shape[2]
    Rs = Bd * Ld

    xp = x_prompt.reshape(Lp, D_MODEL)
    xs = x_sample.reshape(Rs, D_MODEL)
    tab_p = _rope_tables(jnp.arange(Lp))
    tab_s = _rope_tables(jnp.tile(past + jnp.arange(Ld), Bd))
    cache_k2 = jnp.transpose(cache_k, (0, 1, 3, 4, 2)).reshape(depth * n_pool, 512, PAGE_SIZE)
    cache_v2 = jnp.transpose(cache_v, (0, 1, 3, 4, 2)).reshape(depth * n_pool, 512, PAGE_SIZE)
    cache_ik2 = jnp.transpose(cache_idx_k, (0, 1, 3, 2)).reshape(depth * n_pool, IDX_DIM, PAGE_SIZE)
    page_flat = page_table.reshape(-1).astype(I32)
    zeros_c = jnp.zeros((Bp, ML_HEADS, ML_QK_DIM, ML_V_DIM), F32)
    zeros_n = jnp.zeros((Bp, ML_HEADS, ML_QK_DIM), F32)
    zeros_m = jnp.zeros((Bp, ML_HEADS), F32)
    eye_h = jnp.eye(N_HEADS, dtype=BF16)

    outs = {k: [] for k in ("kp", "vp", "ikp", "Cp", "np", "mp", "ks", "vs", "iks", "Cs", "ns", "ms")}
    for l in range(depth):
        W = _layer_weights(l, norm_mix, w_in, q_norm, k_norm, gate_bias, w_branch, w_out, norm_ffn, w_gate_up, w_down)

        def inproj(x, tabs):
            return _inproj(x, tabs, W["g_mix"], W["wmain"], W["wmisc"], W["qn"], W["kn"], W["bias"], W["bd"])

        aq, akf, akb, avf, avb, iq, misc, ikT, mq, mk, mv, mo, gates = inproj(xp, tab_p)
        a_out = _dsa_prompt(aq, iq, misc, ikT, akb, avb)
        gT = jnp.transpose(misc[:, _M_MI:_M_MI + 2 * ML_HEADS])[None]
        m_h, C, n, m = _mlstm(mq, mk, mv, misc, gT, zeros_c, zeros_n, zeros_m, Bp, Lp)
        xp = _merge(xp, a_out, m_h, mo, gates, W["wb0"], W["wb1"], W["wout"])
        xp = _ffn(xp, W["g_ffn"], W["wgu"], W["wd"])
        outs["kp"].append(akf.reshape(Bp, Lp, N_HEADS, HEAD_DIM))
        outs["vp"].append(avf.reshape(Bp, Lp, N_HEADS, HEAD_DIM))
        outs["ikp"].append(misc[:, :IDX_DIM].reshape(Bp, Lp, IDX_DIM))
        outs["Cp"].append(C), outs["np"].append(n), outs["mp"].append(m)

        aq, akf, akb, avf, avb, iq, misc, ikT, mq, mk, mv, mo, gates = inproj(xs, tab_s)
        iqs = iq.reshape(Bd, Ld, N_IDX_HEADS, IDX_DIM).transpose(0, 2, 1, 3).reshape(Bd, N_IDX_HEADS * Ld, IDX_DIM)
        wst = misc[:, _M_IW:_M_IW + N_IDX_HEADS].reshape(Bd, Ld, N_IDX_HEADS).transpose(0, 2, 1)
        wst = jnp.broadcast_to(wst.reshape(Bd, N_IDX_HEADS * Ld, 1), (Bd, N_IDX_HEADS * Ld, LANES))
        pad_rows = lambda t: jnp.pad(t.reshape(Bd, Ld, -1).transpose(0, 2, 1), ((0, 0), (0, 0), (0, LANES - Ld)))
        iknew = pad_rows(misc[:, :IDX_DIM].astype(BF16))
        keys, thr, jsel = _dsa_sample_select(page_flat, iqs, wst, iknew, cache_ik2, l, n_pool, n_pages, Ld)
        qbd = jnp.einsum("bqhd,hg->bhqgd", aq.reshape(Bd, Ld, N_HEADS, HEAD_DIM), eye_h)
        qbd = qbd.reshape(Bd, N_HEADS * Ld, 512)
        a_out = _dsa_sample_attend(page_flat, qbd, keys, thr, jsel, pad_rows(akb), pad_rows(avb),
                                   cache_k2, cache_v2, l, n_pool, n_pages, Ld).reshape(Rs, 512)
        gT = misc[:, _M_MI:_M_MI + 2 * ML_HEADS].reshape(Bd, Ld, 2 * ML_HEADS).transpose(0, 2, 1)
        m_h, C, n, m = _mlstm(mq, mk, mv, misc, gT, state_C[l], state_n[l], state_m[l], Bd, Ld)
        xs = _merge(xs, a_out, m_h, mo, gates, W["wb0"], W["wb1"], W["wout"])
        xs = _ffn(xs, W["g_ffn"], W["wgu"], W["wd"])
        outs["ks"].append(akf.reshape(Bd, Ld, N_HEADS, HEAD_DIM))
        outs["vs"].append(avf.reshape(Bd, Ld, N_HEADS, HEAD_DIM))
        outs["iks"].append(misc[:, :IDX_DIM].reshape(Bd, Ld, IDX_DIM))
        outs["Cs"].append(C), outs["ns"].append(n), outs["ms"].append(m)

    st = lambda k: jnp.stack(outs[k])
    return (xp.reshape(Bp, Lp, D_MODEL), xs.reshape(Bd, Ld, D_MODEL),
            st("kp"), st("vp"), st("ikp"), st("Cp"), st("np"), st("mp"),
            st("ks"), st("vs"), st("iks"), st("Cs"), st("ns"), st("ms"))
```

```python
import functools

import numpy as np
import jax
import jax.numpy as jnp
from jax import lax
from jax.experimental import pallas as pl
from jax.experimental.pallas import tpu as pltpu

D_MODEL = 1024
BRANCH_WIDTH = D_MODEL // 2
HEAD_DIM = 64
N_HEADS = BRANCH_WIDTH // HEAD_DIM
N_IDX_HEADS = 8
IDX_DIM = 64
TOPK_MAX = 256
ROPE_THETA = 500000.0
ROPE_FRACTION = 4
ML_QK_DIM = 64
ML_V_DIM = 128
ML_HEADS = BRANCH_WIDTH // ML_V_DIM
ML_CHUNK = 64
FF_DIM = -(-8 * D_MODEL // (3 * 256)) * 256
PAGE_SIZE = 128
EPS = 1e-6

LANES = 128
VMEM_LIMIT = 56 * 1024 * 1024
INT_MIN = -2 ** 31
NEG_BIG = -1e30
CAND_DEPTH = 12
INSERT_ROWS = 32
BIT_PASS_UNROLL = 8
RING_SLOTS = 3

_C_AQ, _C_IK, _C_IW, _C_MQ, _C_MI, _C_MO = 0, 2048, 2112, 2120, 3144, 3152
_IN_COLS = 5712
_M_IW, _M_MI, _M_MF = 64, 72, 76

F32 = jnp.float32
BF16 = jnp.bfloat16
I32 = jnp.int32


def _cparams(sem):
    return pltpu.CompilerParams(dimension_semantics=sem, vmem_limit_bytes=VMEM_LIMIT)


def _pick_tile(n, prefs):
    for t in prefs:
        if n % t == 0:
            return t
    return n


def _const_spec(shape):
    nd = len(shape)
    return pl.BlockSpec(shape, lambda *a: (0,) * nd)


def _inproj_kernel(x_ref, g_ref, wmain_ref, wmisc_ref, qn_ref, kn_ref, bias_ref, cos_ref, sin_ref, bd_ref,
                   aq_ref, akf_ref, akb_ref, avf_ref, avb_ref, iq_ref, misc_ref, ikT_ref,
                   mq_ref, mk_ref, mv_ref, mo_ref, gates_ref):
    x = x_ref[...]
    tm = x.shape[0]
    xn = (x * lax.rsqrt(jnp.mean(x * x, axis=-1, keepdims=True) + EPS) * g_ref[...]).astype(BF16)
    cosv = cos_ref[...]
    sinv = sin_ref[...]
    lane = lax.broadcasted_iota(I32, (tm, LANES), 1)
    first = (lane % HEAD_DIM) < (HEAD_DIM // ROPE_FRACTION // 2)
    shift = HEAD_DIM // ROPE_FRACTION // 2

    def rope(y, c, s):
        partner = jnp.where(first, pltpu.roll(y, LANES - shift, 1), pltpu.roll(y, shift, 1))
        return y * c + partner * s

    def headnorm(y, gain):
        ms = jnp.dot(y * y, bd_ref[...], preferred_element_type=F32, precision=lax.Precision.HIGHEST)
        return y * lax.rsqrt(ms + EPS) * gain

    def proj(c0, c1):
        return jnp.dot(xn, wmain_ref[:, c0:c1], preferred_element_type=F32)

    yq = proj(0, 512)
    yk = proj(512, 1024)
    yi = proj(1536, 2048)
    for g in range(4):
        sl = slice(g * LANES, (g + 1) * LANES)
        q = rope(headnorm(yq[:, sl], qn_ref[...]), cosv, sinv)
        aq_ref[:, sl] = (q * (HEAD_DIM ** -0.5)).astype(BF16)
        k = rope(headnorm(yk[:, sl], kn_ref[...]), cosv, sinv)
        akf_ref[:, sl] = k
        akb_ref[:, sl] = k.astype(BF16)
        iq_ref[:, sl] = (rope(yi[:, sl], cosv, sinv) * (IDX_DIM ** -0.5)).astype(BF16)
    yv = proj(1024, 1536)
    avf_ref[...] = yv
    avb_ref[...] = yv.astype(BF16)
    mq_ref[...] = proj(2048, 2304)
    mk_ref[...] = proj(2304, 2560)
    mv_ref[...] = proj(2560, 3072)
    mo_ref[...] = proj(3072, 3584)
    gates_ref[...] = proj(3584, 5632)
    ym = jnp.dot(xn, wmisc_ref[...], preferred_element_type=F32) + bias_ref[...]
    is_ik = lane < IDX_DIM
    ym = rope(ym, jnp.where(is_ik, cosv, 1.0), jnp.where(is_ik, sinv, 0.0))
    misc_ref[...] = ym
    ikT_ref[...] = jnp.transpose(ym)[:IDX_DIM, :].astype(BF16)


def _rope_tables(pos):
    half = HEAD_DIM // ROPE_FRACTION // 2
    freqs = ROPE_THETA ** (-jnp.arange(half, dtype=F32) / half)
    ang = pos.astype(F32)[:, None] * freqs[None, :]
    cos, sin = jnp.cos(ang), jnp.sin(ang)
    n = pos.shape[0]
    ones = jnp.ones((n, HEAD_DIM - 2 * half), F32)
    c = jnp.concatenate([cos, cos, ones], axis=1)
    s = jnp.concatenate([-sin, sin, 0.0 * ones], axis=1)
    return jnp.tile(c, (1, LANES // HEAD_DIM)), jnp.tile(s, (1, LANES // HEAD_DIM))


def _inproj(x, pos_tables, g, wmain, wmisc, qn, kn, bias, bd):
    R = x.shape[0]
    tm = _pick_tile(R, (256, 128))
    cos_t, sin_t = pos_tables
    row = lambda w: pl.BlockSpec((tm, w), lambda i: (i, 0))
    out_shapes = [
        jax.ShapeDtypeStruct((R, 512), BF16),
        jax.ShapeDtypeStruct((R, 512), F32),
        jax.ShapeDtypeStruct((R, 512), BF16),
        jax.ShapeDtypeStruct((R, 512), F32),
        jax.ShapeDtypeStruct((R, 512), BF16),
        jax.ShapeDtypeStruct((R, 512), BF16),
        jax.ShapeDtypeStruct((R, LANES), F32),
        jax.ShapeDtypeStruct((IDX_DIM, R), BF16),
        jax.ShapeDtypeStruct((R, 256), F32),
        jax.ShapeDtypeStruct((R, 256), F32),
        jax.ShapeDtypeStruct((R, 512), F32),
        jax.ShapeDtypeStruct((R, 512), F32),
        jax.ShapeDtypeStruct((R, 2048), F32),
    ]
    out_specs = [row(512), row(512), row(512), row(512), row(512), row(512), row(LANES),
                 pl.BlockSpec((IDX_DIM, tm), lambda i: (0, i)),
                 row(256), row(256), row(512), row(512), row(2048)]
    in_specs = [row(D_MODEL), _const_spec((1, D_MODEL)), _const_spec(wmain.shape), _const_spec(wmisc.shape),
                _const_spec((1, LANES)), _const_spec((1, LANES)), _const_spec((1, LANES)),
                row(LANES), row(LANES), _const_spec((LANES, LANES))]
    return pl.pallas_call(
        _inproj_kernel, grid=(R // tm,), in_specs=in_specs, out_specs=out_specs, out_shape=out_shapes,
        compiler_params=_cparams(("parallel",)), name="inproj",
    )(x, g, wmain, wmisc, qn, kn, bias, cos_t, sin_t, bd)


def _f32_key(x):
    bits = pltpu.bitcast(x, I32)
    return bits ^ (lax.shift_right_arithmetic(bits, 31) & 0x7FFFFFFF)


def _key_f32(k):
    return pltpu.bitcast(k ^ (lax.shift_right_arithmetic(k, 31) & 0x7FFFFFFF), F32)


def _count_rows(ref, r0, rows, n_iter, inner, preds, as_key=False):
    def body(c, accs):
        accs = list(accs)
        for u in range(inner):
            k0 = pl.multiple_of((c * inner + u) * LANES, LANES)
            kk = ref[r0:r0 + rows, pl.ds(k0, LANES)]
            if as_key:
                kk = _f32_key(kk)
            for i, pred in enumerate(preds):
                accs[i] = accs[i] + pred(kk, k0).astype(I32)
        return tuple(accs)
    zero = jnp.zeros((rows, LANES), I32)
    if isinstance(n_iter, int) and n_iter <= 16:
        accs = (zero,) * len(preds)
        for c in range(n_iter):
            accs = body(c, accs)
    else:
        accs = lax.fori_loop(0, n_iter, body, (zero,) * len(preds))
    return [jnp.sum(a, axis=1, keepdims=True) for a in accs]


def _kth_largest(ref, blocks, rows, n_iter, inner, as_key=False):
    def bit_pass(bi, t_us):
        bit = lax.shift_left(jnp.int32(1), 31 - bi)
        out = []
        for (r0, kprime), t_u in zip(blocks, t_us):
            cand = jnp.broadcast_to((t_u | bit) ^ INT_MIN, (rows, LANES))
            cnt, = _count_rows(ref, r0, rows, n_iter, inner, [lambda kk, k0, cand=cand: kk >= cand], as_key)
            out.append(jnp.where(cnt >= kprime, t_u | bit, t_u))
        return tuple(out)

    unroll = BIT_PASS_UNROLL if isinstance(n_iter, int) else 1
    t_us = lax.fori_loop(0, 32, bit_pass, tuple(jnp.zeros((rows, 1), I32) for _ in blocks), unroll=unroll)
    return [t_u ^ INT_MIN for t_u in t_us]


def _tie_cutoff(sc_ref, r0, rows, n_iter, inner, thr, need, n_idx_bits):
    thr_b = jnp.broadcast_to(thr, (rows, LANES))
    lane = lax.broadcasted_iota(I32, (rows, LANES), 1)

    def idx_pass(bi, j):
        bit = lax.shift_left(jnp.int32(1), n_idx_bits - 1 - bi)
        cand = jnp.broadcast_to(j | bit, (rows, LANES))
        cnt, = _count_rows(sc_ref, r0, rows, n_iter, inner,
                           [lambda kk, k0: (kk == thr_b) & ((lane + k0) < cand)])
        return jnp.where(cnt < need, j | bit, j)

    return lax.fori_loop(0, n_idx_bits, idx_pass, jnp.zeros((rows, 1), I32))


def _tie_walk(sc_ref, r0, rows, n_iter, inner, thr_b, need, max_need):
    lane = lax.broadcasted_iota(I32, (rows, LANES), 1)
    far = jnp.float32(3e38)

    def step(t, j_cur):
        j_b = jnp.broadcast_to(j_cur, (rows, LANES))

        def body(c, acc):
            for u in range(inner):
                k0 = pl.multiple_of((c * inner + u) * LANES, LANES)
                idx = (lane + k0).astype(F32)
                hit = (sc_ref[r0:r0 + rows, pl.ds(k0, LANES)] == thr_b) & (idx > j_b)
                acc = jnp.minimum(acc, jnp.where(hit, idx, far))
            return acc

        acc = lax.fori_loop(0, n_iter, body, jnp.full((rows, LANES), far, F32))
        return jnp.where(t < need, jnp.min(acc, axis=1, keepdims=True), j_cur)

    return lax.fori_loop(0, max_need, step, jnp.full((rows, 1), -1.0, F32)).astype(I32)


def _lane_top_candidates(sc_ref, cand_ref, r0, rows, n_iter, inner, depth):
    sub = INSERT_ROWS if rows % INSERT_ROWS == 0 else 8

    def row_block(rb, carry):
        rs = pl.multiple_of(r0 + rb * sub, sub)

        def body(c, tops):
            tops = list(tops)
            for u in range(inner):
                k0 = pl.multiple_of((c * inner + u) * LANES, LANES)
                x = sc_ref[pl.ds(rs, sub), pl.ds(k0, LANES)]
                for i in range(depth):
                    hi = jnp.maximum(tops[i], x)
                    x = jnp.minimum(tops[i], x)
                    tops[i] = hi
            return tuple(tops)

        tops = lax.fori_loop(0, n_iter, body, (jnp.full((sub, LANES), -jnp.inf, F32),) * depth)
        for i in range(depth):
            cand_ref[pl.ds(rs, sub), i * LANES:(i + 1) * LANES] = _f32_key(tops[i])
        return carry

    lax.fori_loop(0, rows // sub, row_block, 0)


def _topk_select(sc_ref, cand_ref, thr_ref, jsel_ref, blocks, rows, n_iter, inner, n_idx_bits):
    for r0, _ in blocks:
        _lane_top_candidates(sc_ref, cand_ref, r0, rows, n_iter, inner, CAND_DEPTH)
    thrs = _kth_largest(cand_ref, blocks, rows, CAND_DEPTH, 1)
    for (r0, kprime), thr_key in zip(blocks, thrs):
        thr_b = jnp.broadcast_to(_key_f32(thr_key), (rows, LANES))
        n_gt, n_ge = _count_rows(sc_ref, r0, rows, n_iter, inner,
                                 [lambda kk, k0: kk > thr_b, lambda kk, k0: kk >= thr_b])
        thr_ref[r0:r0 + rows, :] = thr_b
        jsel_ref[r0:r0 + rows, :] = jnp.full((rows, LANES), 2 ** 31 - 1, I32)
        wrong = jnp.max(((n_gt >= kprime) | (n_ge < kprime)).astype(I32))
        excess = n_ge > kprime
        need = jnp.where(excess, kprime - n_gt, 0)
        max_need = jnp.max(need)

        @pl.when((wrong == 0) & (max_need > 0))
        def _ties_only():
            @pl.when(max_need <= n_idx_bits)
            def _walk():
                j_sel = _tie_walk(sc_ref, r0, rows, n_iter, inner, thr_b, need, max_need)
                jsel_ref[r0:r0 + rows, :] = jnp.broadcast_to(jnp.where(excess, j_sel, 2 ** 31 - 1), (rows, LANES))

            @pl.when(max_need > n_idx_bits)
            def _bisect():
                j_sel = _tie_cutoff(sc_ref, r0, rows, n_iter, inner, thr_b[:, 0:1], kprime - n_gt, n_idx_bits)
                jsel_ref[r0:r0 + rows, :] = jnp.broadcast_to(jnp.where(excess, j_sel, 2 ** 31 - 1), (rows, LANES))

        @pl.when(wrong > 0)
        def _full_search():
            t_key, = _kth_largest(sc_ref, [(r0, kprime)], rows, n_iter, inner, as_key=True)
            t = _key_f32(t_key)
            t_b = jnp.broadcast_to(t, (rows, LANES))
            gt, = _count_rows(sc_ref, r0, rows, n_iter, inner, [lambda kk, k0: kk > t_b])
            j_sel = _tie_cutoff(sc_ref, r0, rows, n_iter, inner, t, kprime - gt, n_idx_bits)
            thr_ref[r0:r0 + rows, :] = t_b
            jsel_ref[r0:r0 + rows, :] = jnp.broadcast_to(j_sel, (rows, LANES))


def _dsa_prompt_kernel(qi_ref, kj_ref, last_ref,
                       aq_ref, iq_ref, misc_ref, ikT_ref, k_ref, v_ref, out_ref,
                       sc_sc, cand_sc, thr_sc, jsel_sc, qpair_sc, bias_sc, lm_sc, alpha_sc, m_sc, acc_sc,
                       *, tq, kc, topk, seq_len):
    s_id = pl.program_id(0)
    qi = qi_ref[s_id]
    kj = kj_ref[s_id]
    n_idx_bits = int(seq_len).bit_length()
    n_pairs = N_HEADS // 2
    n_cb = kc // LANES

    @pl.when(kj == 0)
    def _select():
        q_pos = qi * tq + lax.broadcasted_iota(I32, (tq, 1), 0)
        n_kc = (qi * tq + tq + kc - 1) // kc
        w_all = misc_ref[:, _M_IW:_M_IW + N_IDX_HEADS] * (N_IDX_HEADS ** -0.5)

        def score_body(c, carry):
            k0 = pl.multiple_of(c * kc, kc)
            ikc = ikT_ref[:, pl.ds(k0, kc)]
            acc = jnp.zeros((tq, kc), F32)
            for h in range(N_IDX_HEADS):
                s = jnp.dot(iq_ref[:, h * IDX_DIM:(h + 1) * IDX_DIM], ikc, preferred_element_type=F32)
                acc = acc + jnp.maximum(s, 0.0) * w_all[:, h:h + 1]
            k_pos = k0 + lax.broadcasted_iota(I32, (1, kc), 1)
            sc_sc[:, pl.ds(k0, kc)] = jnp.where(k_pos <= q_pos, acc, -jnp.inf)
            return carry

        lax.fori_loop(0, n_kc, score_body, 0)
        kprime = jnp.minimum(topk, q_pos + 1)
        rb = min(tq, LANES)
        _topk_select(sc_sc, cand_sc, thr_sc, jsel_sc, [(r0, kprime[r0:r0 + rb]) for r0 in range(0, tq, rb)],
                     rb, n_kc, kc // LANES, n_idx_bits)
        pair_half = lax.broadcasted_iota(I32, (tq, LANES), 1) // HEAD_DIM
        for j in range(n_pairs):
            q2 = aq_ref[:, j * LANES:(j + 1) * LANES]
            for half in range(2):
                qpair_sc[j, half * tq:(half + 1) * tq, :] = jnp.where(pair_half == half, q2, jnp.zeros_like(q2))
        m_sc[...] = jnp.full(m_sc.shape, NEG_BIG, F32)
        acc_sc[...] = jnp.zeros(acc_sc.shape, F32)

    k0 = pl.multiple_of(kj * kc, kc)
    thr = thr_sc[...]
    j_sel = jsel_sc[...]
    lane = lax.broadcasted_iota(I32, (1, LANES), 1)
    for c in range(n_cb):
        kk = sc_sc[:, pl.ds(k0 + c * LANES, LANES)]
        sel = (kk > thr) | ((kk == thr) & ((k0 + c * LANES + lane) <= j_sel))
        bias_sc[:, c * LANES:(c + 1) * LANES] = jnp.where(sel, 0.0, NEG_BIG)

    for j in range(n_pairs):
        s = lax.dot_general(qpair_sc[j], k_ref[:, j * LANES:(j + 1) * LANES], (((1,), (1,)), ((), ())),
                            preferred_element_type=F32)
        for half in range(2):
            h = 2 * j + half
            lm = [s[half * tq:(half + 1) * tq, c * LANES:(c + 1) * LANES] + bias_sc[:, c * LANES:(c + 1) * LANES]
                  for c in range(n_cb)]
            mx = lm[0]
            for c in range(1, n_cb):
                mx = jnp.maximum(mx, lm[c])
            m_old = m_sc[h]
            m_new = jnp.maximum(m_old, jnp.max(mx, axis=1, keepdims=True))
            for c in range(n_cb):
                lm_sc[h, :, c * LANES:(c + 1) * LANES] = lm[c] - m_new
            alpha_sc[h] = jnp.exp(m_old - m_new)
            m_sc[h] = m_new

    ones = jnp.ones((kc, LANES), BF16)
    for j in range(n_pairs):
        p = jnp.exp(lm_sc[2 * j:2 * j + 2].reshape(2 * tq, kc)).astype(BF16)
        v_aug = jnp.concatenate([v_ref[:, j * LANES:(j + 1) * LANES], ones], axis=1)
        pv = jnp.dot(p, v_aug, preferred_element_type=F32)
        for half in range(2):
            h = 2 * j + half
            alpha = alpha_sc[h]
            acc_sc[h] = jnp.concatenate([alpha, alpha], axis=1) * acc_sc[h] + pv[half * tq:(half + 1) * tq]

    @pl.when(last_ref[s_id] == 1)
    def _finish():
        for h in range(N_HEADS):
            own = slice((h % 2) * HEAD_DIM, (h % 2 + 1) * HEAD_DIM)
            den = slice(LANES + (h % 2) * HEAD_DIM, LANES + (h % 2 + 1) * HEAD_DIM)
            out_ref[:, h * HEAD_DIM:(h + 1) * HEAD_DIM] = (acc_sc[h][:, own] / acc_sc[h][:, den]).astype(out_ref.dtype)


def _dsa_prompt(aq, iq, misc, ikT, akb, avb):
    L = aq.shape[0]
    tq = _pick_tile(L, (256, 128))
    kc = _pick_tile(L, (1024, 512, 256, 128))
    topk = min(TOPK_MAX, L // 4)
    qi, kj, last = [], [], []
    for i in range(L // tq):
        n_kc = ((i + 1) * tq + kc - 1) // kc
        for j in range(n_kc):
            qi.append(i), kj.append(j), last.append(int(j == n_kc - 1))
    qi, kj, last = (jnp.asarray(np.asarray(a, np.int32)) for a in (qi, kj, last))
    qrow = lambda w: pl.BlockSpec((tq, w), lambda s, qi, kj, last: (qi[s], 0))
    krow = pl.BlockSpec((kc, 512), lambda s, qi, kj, last: (kj[s], 0))
    grid_spec = pltpu.PrefetchScalarGridSpec(
        num_scalar_prefetch=3, grid=(int(qi.shape[0]),),
        in_specs=[qrow(512), qrow(512), qrow(LANES), pl.BlockSpec((IDX_DIM, L), lambda s, qi, kj, last: (0, 0)),
                  krow, krow],
        out_specs=qrow(512),
        scratch_shapes=[pltpu.VMEM((tq, L), F32), pltpu.VMEM((tq, CAND_DEPTH * LANES), I32),
                        pltpu.VMEM((tq, LANES), F32), pltpu.VMEM((tq, LANES), I32),
                        pltpu.VMEM((N_HEADS // 2, 2 * tq, LANES), BF16), pltpu.VMEM((tq, kc), F32),
                        pltpu.VMEM((N_HEADS, tq, kc), F32), pltpu.VMEM((N_HEADS, tq, LANES), F32),
                        pltpu.VMEM((N_HEADS, tq, LANES), F32), pltpu.VMEM((N_HEADS, tq, 2 * LANES), F32)])
    return pl.pallas_call(
        functools.partial(_dsa_prompt_kernel, tq=tq, kc=kc, topk=topk, seq_len=L),
        grid_spec=grid_spec, out_shape=jax.ShapeDtypeStruct((L, 512), BF16),
        compiler_params=_cparams(("arbitrary",)), name="dsa_prompt",
    )(qi, kj, last, aq, iq, misc, ikT, akb, avb)


def _dsa_sample_select_kernel(pt_ref, iqs_ref, wst_ref, iknew_ref, *rest, pages, n_pages, topk, past, nq, batch):
    page_refs = rest[:pages]
    keys_ref, thr_ref, jsel_ref, cand_sc = rest[pages:pages + 4]
    b = pl.program_id(0)
    j = pl.program_id(1)
    last_j = n_pages // pages - 1
    iqs = iqs_ref[0]
    w = wst_ref[0][:, 0:1] * (N_IDX_HEADS ** -0.5)
    row0 = pl.multiple_of(b * nq, nq)

    def scores(ik_t):
        s = jnp.dot(iqs, ik_t, preferred_element_type=F32)
        t = jnp.maximum(s, 0.0) * w
        acc = jnp.zeros((nq, ik_t.shape[1]), F32)
        for h in range(N_IDX_HEADS):
            acc = acc + t[h * nq:(h + 1) * nq, :]
        return acc

    k0 = pl.multiple_of(j * (pages * PAGE_SIZE), pages * PAGE_SIZE)
    ik_pages = jnp.concatenate([r[0] for r in page_refs], axis=1).astype(BF16)
    keys_ref[pl.ds(row0, nq), pl.ds(k0, pages * PAGE_SIZE)] = scores(ik_pages)

    @pl.when(j == last_j)
    def _new_tokens():
        qrow = lax.broadcasted_iota(I32, (nq, LANES), 0)
        lane = lax.broadcasted_iota(I32, (nq, LANES), 1)
        keys_ref[pl.ds(row0, nq), past:past + LANES] = jnp.where(lane <= qrow, scores(iknew_ref[0]), -jnp.inf)

    @pl.when((j == last_j) & (b == batch - 1))
    def _finish():
        rows = batch * nq
        rb = min(rows, LANES)
        qrow = lax.broadcasted_iota(I32, (rb, 1), 0) % nq
        _topk_select(keys_ref, cand_sc, thr_ref, jsel_ref,
                     [(r0, jnp.minimum(topk, past + 1 + qrow)) for r0 in range(0, rows, rb)],
                     rb, n_pages + 1, 1, int(past + LANES).bit_length())


def _dsa_sample_select(page_flat, iqs, wst, iknew, cache_ik2, layer, n_pool, n_pages, nq):
    Bd = iqs.shape[0]
    pages = _pick_tile(n_pages, (32, 16, 8, 4, 2, 1))
    past = n_pages * PAGE_SIZE
    topk = min(TOPK_MAX, (past + nq) // 4)
    base = layer * n_pool
    rows = Bd * nq

    def page_spec(p):
        return pl.BlockSpec((1, IDX_DIM, PAGE_SIZE),
                            lambda b, j, pt: (base + pt[b * n_pages + j * pages + p], 0, 0))

    per_b = lambda shape: pl.BlockSpec((1,) + shape, lambda b, j, pt: (b, 0, 0))
    whole = lambda shape: pl.BlockSpec(shape, lambda b, j, pt: (0, 0))
    grid_spec = pltpu.PrefetchScalarGridSpec(
        num_scalar_prefetch=1, grid=(Bd, n_pages // pages),
        in_specs=[per_b((N_IDX_HEADS * nq, IDX_DIM)), per_b((N_IDX_HEADS * nq, LANES)), per_b((IDX_DIM, LANES))]
                 + [page_spec(p) for p in range(pages)],
        out_specs=[whole((rows, past + LANES)), whole((rows, LANES)), whole((rows, LANES))],
        scratch_shapes=[pltpu.VMEM((rows, CAND_DEPTH * LANES), I32)])
    return pl.pallas_call(
        functools.partial(_dsa_sample_select_kernel, pages=pages, n_pages=n_pages, topk=topk, past=past, nq=nq,
                          batch=Bd),
        grid_spec=grid_spec,
        out_shape=[jax.ShapeDtypeStruct((rows, past + LANES), F32), jax.ShapeDtypeStruct((rows, LANES), F32),
                   jax.ShapeDtypeStruct((rows, LANES), I32)],
        compiler_params=_cparams(("arbitrary", "arbitrary")), name="dsa_sample_select",
    )(page_flat, iqs, wst, iknew, *([cache_ik2] * pages))


def _dsa_sample_attend_kernel(pt_ref, qbd_ref, keys_ref, thr_ref, jsel_ref, knew_ref, vnew_ref, ck_ref, cv_ref,
                              out_ref, kbuf, vbuf, sem, m_sc, l_sc, acc_sc,
                              *, pages, n_steps, past, nq, n_pages, base, batch):
    b = pl.program_id(0)
    j = pl.program_id(1)
    rows = N_HEADS * nq
    total = batch * n_steps

    def page_copies(t, slot):
        first = (t // n_steps) * n_pages + (t % n_steps) * pages
        out = []
        for p in range(pages):
            page = base + pt_ref[first + p]
            out.append(pltpu.make_async_copy(ck_ref.at[page], kbuf.at[slot, p], sem.at[slot, 0, p]))
            out.append(pltpu.make_async_copy(cv_ref.at[page], vbuf.at[slot, p], sem.at[slot, 1, p]))
        return out
    qbd = qbd_ref[0]
    thr = thr_ref[:, 0:1]
    j_sel = jsel_ref[:, 0:1]

    @pl.when(j == 0)
    def _init():
        m_sc[...] = jnp.full(m_sc.shape, NEG_BIG, F32)
        l_sc[...] = jnp.zeros(l_sc.shape, F32)
        acc_sc[...] = jnp.zeros(acc_sc.shape, F32)

    def attend(k_list, v_list, k0):
        width = LANES * len(k_list)
        kk = keys_ref[:, pl.ds(k0, width)]
        k_pos = k0 + lax.broadcasted_iota(I32, (1, width), 1)
        sel = (kk > thr) | ((kk == thr) & (k_pos <= j_sel))
        sel = jnp.concatenate([sel.astype(I32)] * N_HEADS, axis=0) > 0
        logits = jnp.concatenate([jnp.dot(qbd, kp, preferred_element_type=F32) for kp in k_list], axis=1)
        lm = jnp.where(sel, logits, NEG_BIG)
        m_old = m_sc[:, 0:1]
        m_new = jnp.maximum(m_old, jnp.max(lm, axis=1, keepdims=True))
        p = jnp.exp(lm - m_new)
        alpha = jnp.exp(m_old - m_new)
        l_sc[...] = jnp.broadcast_to(alpha * l_sc[:, 0:1] + jnp.sum(p, axis=1, keepdims=True), (rows, LANES))
        pv = jnp.zeros((rows, 512), F32)
        for i, vp in enumerate(v_list):
            pv = pv + lax.dot_general(p[:, i * LANES:(i + 1) * LANES].astype(BF16), vp, (((1,), (1,)), ((), ())),
                                      preferred_element_type=F32)
        acc_sc[...] = alpha * acc_sc[...] + pv
        m_sc[...] = jnp.broadcast_to(m_new, (rows, LANES))

    @pl.when((b == 0) & (j == 0))
    def _prime():
        for t0 in range(min(RING_SLOTS - 1, total)):
            for c in page_copies(t0, t0):
                c.start()

    @pl.when(j < n_steps)
    def _past():
        t = b * n_steps + j

        @pl.when(t + (RING_SLOTS - 1) < total)
        def _prefetch():
            ahead = t + (RING_SLOTS - 1)
            for c in page_copies(ahead, ahead % RING_SLOTS):
                c.start()

        slot = t % RING_SLOTS
        for c in page_copies(t, slot):
            c.wait()
        attend([kbuf[slot, p].astype(BF16) for p in range(pages)], [vbuf[slot, p].astype(BF16) for p in range(pages)],
               pl.multiple_of(j * (pages * PAGE_SIZE), pages * PAGE_SIZE))

    @pl.when(j == n_steps)
    def _new():
        attend([knew_ref[0]], [vnew_ref[0]], past)
        for h in range(N_HEADS):
            hs = slice(h * HEAD_DIM, (h + 1) * HEAD_DIM)
            rs = slice(h * nq, (h + 1) * nq)
            out_ref[0, :, hs] = (acc_sc[rs, hs] / l_sc[rs, 0:1]).astype(out_ref.dtype)


def _dsa_sample_attend(page_flat, qbd, keys, thr, jsel, knew, vnew, cache_k2, cache_v2, layer, n_pool, n_pages, nq):
    Bd = qbd.shape[0]
    pages = _pick_tile(n_pages, (16, 8, 4, 2, 1))
    n_steps = n_pages // pages
    past = n_pages * PAGE_SIZE
    base = layer * n_pool
    rows = N_HEADS * nq

    per_b = lambda shape: pl.BlockSpec((1,) + shape, lambda b, j, pt: (b, 0, 0))
    per_q = lambda w: pl.BlockSpec((nq, w), lambda b, j, pt: (b, 0))
    in_hbm = pl.BlockSpec(memory_space=pl.ANY)
    grid_spec = pltpu.PrefetchScalarGridSpec(
        num_scalar_prefetch=1, grid=(Bd, n_steps + 1),
        in_specs=[per_b((rows, 512)), per_q(past + LANES), per_q(LANES), per_q(LANES),
                  per_b((512, LANES)), per_b((512, LANES)), in_hbm, in_hbm],
        out_specs=per_b((nq, 512)),
        scratch_shapes=[pltpu.VMEM((RING_SLOTS, pages, 512, PAGE_SIZE), F32),
                        pltpu.VMEM((RING_SLOTS, pages, 512, PAGE_SIZE), F32),
                        pltpu.SemaphoreType.DMA((RING_SLOTS, 2, pages)),
                        pltpu.VMEM((rows, LANES), F32), pltpu.VMEM((rows, LANES), F32), pltpu.VMEM((rows, 512), F32)])
    return pl.pallas_call(
        functools.partial(_dsa_sample_attend_kernel, pages=pages, n_steps=n_steps, past=past, nq=nq,
                          n_pages=n_pages, base=base, batch=Bd),
        grid_spec=grid_spec, out_shape=jax.ShapeDtypeStruct((Bd, nq, 512), BF16),
        compiler_params=_cparams(("arbitrary", "arbitrary")), name="dsa_sample_attend",
    )(page_flat, qbd, keys, thr, jsel, knew, vnew, cache_k2, cache_v2)


def _log_sigmoid(x):
    return jnp.minimum(x, 0.0) - jnp.log1p(jnp.exp(-jnp.abs(x)))


def _mlstm_kernel(q_ref, k_ref, v_ref, misc_ref, gT_ref, c0_ref, n0_ref, m0_ref,
                  h_ref, c_out_ref, n_out_ref, m_out_ref, c_sc, n_sc, m_sc, *, lc, group):
    step = pl.program_id(1)
    hp = lax.Precision.HIGHEST

    @pl.when(step == 0)
    def _init():
        c_sc[...] = c0_ref[0]
        n_sc[...] = n0_ref[0]
        m_sc[...] = m0_ref[0]

    nh = ML_HEADS
    ti = lax.broadcasted_iota(I32, (lc, lc), 0)
    si = lax.broadcasted_iota(I32, (lc, lc), 1)
    tri_all = lax.broadcasted_iota(I32, (nh * lc, lc), 1) <= lax.broadcasted_iota(I32, (nh * lc, lc), 0) % lc

    def per_head_rows(vals):
        return jnp.concatenate([jnp.broadcast_to(x, (lc, x.shape[1])) for x in vals], axis=0)

    def stack(vals):
        return jnp.concatenate(vals, axis=0)

    for c in range(group):
        rs = slice(c * lc, (c + 1) * lc)
        heads = range(nh)
        q = [q_ref[rs, h * ML_QK_DIM:(h + 1) * ML_QK_DIM] * (ML_QK_DIM ** -0.5) for h in heads]
        k = [k_ref[rs, h * ML_QK_DIM:(h + 1) * ML_QK_DIM] for h in heads]
        v = [v_ref[rs, h * ML_V_DIM:(h + 1) * ML_V_DIM] for h in heads]
        i_col = stack([misc_ref[rs, _M_MI + h:_M_MI + h + 1] for h in heads])
        lf_colm = _log_sigmoid(misc_ref[rs, _M_MF:_M_MF + nh])
        i_rowm = gT_ref[0, 0:nh, rs]
        lf_rowm = _log_sigmoid(gT_ref[0, nh:2 * nh, rs])
        m = [m_sc[h][:, 0:1] for h in heads]
        m_all = per_head_rows(m)
        lf_row = per_head_rows([lf_rowm[h:h + 1, :] for h in heads])
        i_row = per_head_rows([i_rowm[h:h + 1, :] for h in heads])
        b_col = jnp.sum(jnp.where(tri_all, lf_row, 0.0), axis=1, keepdims=True)
        b_rowm = stack([jnp.sum(jnp.where(ti <= si, lf_colm[:, h:h + 1], 0.0), axis=0, keepdims=True)
                        for h in heads])
        b_row = per_head_rows([b_rowm[h:h + 1, :] for h in heads])
        dmat = jnp.where(tri_all, b_col - b_row + i_row, -jnp.inf)
        inter = b_col + m_all
        mt = jnp.maximum(inter, jnp.max(dmat, axis=1, keepdims=True))
        s = stack([lax.dot_general(q[h], k[h], (((1,), (1,)), ((), ())), preferred_element_type=F32, precision=hp)
                   for h in heads]) * jnp.exp(dmat - mt)
        w = jnp.exp(inter - mt)
        num = (stack([jnp.dot(s[h * lc:(h + 1) * lc], v[h], preferred_element_type=F32, precision=hp) for h in heads])
               + w * stack([jnp.dot(q[h], c_sc[h], preferred_element_type=F32, precision=hp) for h in heads]))
        den = (jnp.sum(s, axis=1, keepdims=True)
               + w * stack([jnp.sum(q[h] * n_sc[h], axis=1, keepdims=True) for h in heads]))
        h_out = num / jnp.maximum(jnp.abs(den), jnp.exp(-mt))
        bl = jnp.sum(lf_rowm, axis=1, keepdims=True)
        m_vec = stack(m)
        m_new = jnp.maximum(bl + m_vec, jnp.max(bl - b_rowm + i_rowm, axis=1, keepdims=True))
        decay = jnp.exp(bl + m_vec - m_new)
        bl_all = per_head_rows([bl[h:h + 1, :] for h in heads])
        m_new_all = per_head_rows([m_new[h:h + 1, :] for h in heads])
        kw = jnp.exp(bl_all - b_col + i_col - m_new_all) * stack(k)
        for h in heads:
            hr = slice(h * lc, (h + 1) * lc)
            h_ref[rs, h * ML_V_DIM:(h + 1) * ML_V_DIM] = h_out[hr]
            c_sc[h] = decay[h:h + 1, :] * c_sc[h] + lax.dot_general(kw[hr], v[h], (((0,), (0,)), ((), ())),
                                                                    preferred_element_type=F32, precision=hp)
            n_sc[h] = decay[h:h + 1, :] * n_sc[h] + jnp.sum(kw[hr], axis=0, keepdims=True)
            m_sc[h] = jnp.broadcast_to(m_new[h:h + 1, :], (1, LANES))

    @pl.when(step == pl.num_programs(1) - 1)
    def _finish():
        c_out_ref[0] = c_sc[...]
        n_out_ref[0] = n_sc[...]
        m_out_ref[0] = m_sc[...]


def _mlstm(q, k, v, misc, gT, c0, n0, m0, batch, length):
    lc = int(np.gcd(length, ML_CHUNK))
    nc = length // lc
    group = _pick_tile(nc, (4, 2, 1))
    if (group * lc) % LANES != 0 and nc != group:
        group = nc
    rows = group * lc
    steps = nc // group
    row = lambda w: pl.BlockSpec((rows, w), lambda b, s: (b * steps + s, 0))
    st = lambda shape: pl.BlockSpec((1,) + shape, lambda b, s: (b,) + (0,) * len(shape))
    n0 = n0.reshape(batch, ML_HEADS, 1, ML_QK_DIM)
    m0 = jnp.broadcast_to(m0.reshape(batch, ML_HEADS, 1, 1), (batch, ML_HEADS, 1, LANES))
    h, c, n, m = pl.pallas_call(
        functools.partial(_mlstm_kernel, lc=lc, group=group),
        grid=(batch, steps),
        in_specs=[row(256), row(256), row(512), row(LANES),
                  pl.BlockSpec((1, 2 * ML_HEADS, rows), lambda b, s: (b, 0, s)),
                  st((ML_HEADS, ML_QK_DIM, ML_V_DIM)), st((ML_HEADS, 1, ML_QK_DIM)), st((ML_HEADS, 1, LANES))],
        out_specs=[row(512), st((ML_HEADS, ML_QK_DIM, ML_V_DIM)), st((ML_HEADS, 1, ML_QK_DIM)),
                   st((ML_HEADS, 1, LANES))],
        out_shape=[jax.ShapeDtypeStruct((batch * length, 512), F32),
                   jax.ShapeDtypeStruct((batch, ML_HEADS, ML_QK_DIM, ML_V_DIM), F32),
                   jax.ShapeDtypeStruct((batch, ML_HEADS, 1, ML_QK_DIM), F32),
                   jax.ShapeDtypeStruct((batch, ML_HEADS, 1, LANES), F32)],
        scratch_shapes=[pltpu.VMEM((ML_HEADS, ML_QK_DIM, ML_V_DIM), F32), pltpu.VMEM((ML_HEADS, 1, ML_QK_DIM), F32),
                        pltpu.VMEM((ML_HEADS, 1, LANES), F32)],
        compiler_params=_cparams(("arbitrary", "arbitrary")), name="mlstm",
    )(q, k, v, misc, gT, c0, n0, m0)
    return h, c, n.reshape(batch, ML_HEADS, ML_QK_DIM), m[:, :, 0, 0]


def _merge_kernel(x_ref, a_ref, mh_ref, mo_ref, gates_ref, wb0_ref, wb1_ref, wout_ref, out_ref):
    m_out = (jax.nn.sigmoid(mo_ref[...]) * mh_ref[...]).astype(BF16)
    y0 = jnp.dot(a_ref[...], wb0_ref[...], preferred_element_type=F32)
    y1 = jnp.dot(m_out, wb1_ref[...], preferred_element_type=F32)
    merged = (jax.nn.sigmoid(gates_ref[:, :D_MODEL]) * y0 + jax.nn.sigmoid(gates_ref[:, D_MODEL:]) * y1)
    out_ref[...] = x_ref[...] + jnp.dot(merged.astype(BF16), wout_ref[...], preferred_element_type=F32)


def _merge(x, a_out, m_h, mo, gates, wb0, wb1, wout):
    R = x.shape[0]
    tm = _pick_tile(R, (512, 256, 128))
    row = lambda w: pl.BlockSpec((tm, w), lambda i: (i, 0))
    return pl.pallas_call(
        _merge_kernel, grid=(R // tm,),
        in_specs=[row(D_MODEL), row(512), row(512), row(512), row(2 * D_MODEL),
                  _const_spec(wb0.shape), _const_spec(wb1.shape), _const_spec(wout.shape)],
        out_specs=row(D_MODEL), out_shape=jax.ShapeDtypeStruct((R, D_MODEL), F32),
        compiler_params=_cparams(("parallel",)), name="merge",
    )(x, a_out, m_h, mo, gates, wb0, wb1, wout)


def _ffn_kernel(x_ref, g_ref, wgu_ref, wd_ref, out_ref, *, fc):
    x = x_ref[...]
    xn = (x * lax.rsqrt(jnp.mean(x * x, axis=-1, keepdims=True) + EPS) * g_ref[...]).astype(BF16)
    acc = x
    for f in range(0, FF_DIM, fc):
        gate = jnp.dot(xn, wgu_ref[:, f:f + fc], preferred_element_type=F32)
        up = jnp.dot(xn, wgu_ref[:, FF_DIM + f:FF_DIM + f + fc], preferred_element_type=F32)
        hidden = (gate * jax.nn.sigmoid(gate) * up).astype(BF16)
        acc = acc + jnp.dot(hidden, wd_ref[f:f + fc, :], preferred_element_type=F32)
    out_ref[...] = acc


def _ffn(x, g, wgu, wd):
    R = x.shape[0]
    tm = _pick_tile(R, (512, 256, 128))
    row = pl.BlockSpec((tm, D_MODEL), lambda i: (i, 0))
    return pl.pallas_call(
        functools.partial(_ffn_kernel, fc=256), grid=(R // tm,),
        in_specs=[row, _const_spec((1, D_MODEL)), _const_spec(wgu.shape), _const_spec(wd.shape)],
        out_specs=row, out_shape=jax.ShapeDtypeStruct((R, D_MODEL), F32),
        compiler_params=_cparams(("parallel",)), name="ffn",
    )(x, g, wgu, wd)


def _layer_weights(l, norm_mix, w_in, q_norm, k_norm, gate_bias, w_branch, w_out, norm_ffn, w_gate_up, w_down):
    w = w_in[l]
    wmain = jnp.concatenate([w[:, _C_AQ:_C_IK], w[:, _C_MQ:_C_MI], w[:, _C_MO:]], axis=1).astype(BF16)
    wmisc = jnp.concatenate([w[:, _C_IK:_C_MQ], w[:, _C_MI:_C_MO],
                             jnp.zeros((D_MODEL, LANES - 80), F32)], axis=1).astype(BF16)
    bias = jnp.concatenate([jnp.zeros((_M_MI,), F32), gate_bias[l], jnp.zeros((LANES - 80,), F32)])[None, :]
    head_id = np.arange(LANES) // HEAD_DIM
    bd = jnp.asarray((head_id[:, None] == head_id[None, :]).astype(np.float32) / HEAD_DIM)
    return dict(
        g_mix=norm_mix[l][None, :], wmain=wmain, wmisc=wmisc,
        qn=jnp.tile(q_norm[l], LANES // HEAD_DIM)[None, :], kn=jnp.tile(k_norm[l], LANES // HEAD_DIM)[None, :],
        bias=bias, bd=bd, wb0=w_branch[l, 0].astype(BF16), wb1=w_branch[l, 1].astype(BF16),
        wout=w_out[l].astype(BF16), g_ffn=norm_ffn[l][None, :], wgu=w_gate_up[l].astype(BF16),
        wd=w_down[l].astype(BF16))


def kernel(x_prompt, x_sample, cache_k, cache_v, cache_idx_k, state_C, state_n, state_m, page_table, norm_mix, w_in, q_norm, k_norm, gate_bias, w_branch, w_out, norm_ffn, w_gate_up, w_down):
    Bp, Lp, _ = x_prompt.shape
    Bd, Ld, _ = x_sample.shape
    depth = w_in.shape[0]
    n_pool = cache_k.shape[1]
    n_pages = page_table.shape[1]
    past = n_pages * PAGE_SIZE
    assert Bp == 1 and PAGE_SIZE == cache_k.shape[2]
    Rs = Bd * Ld

    xp = x_prompt.reshape(Lp, D_MODEL)
    xs = x_sample.reshape(Rs, D_MODEL)
    tab_p = _rope_tables(jnp.arange(Lp))
    tab_s = _rope_tables(jnp.tile(past + jnp.arange(Ld), Bd))
    cache_k2 = jnp.transpose(cache_k, (0, 1, 3, 4, 2)).reshape(depth * n_pool, 512, PAGE_SIZE)
    cache_v2 = jnp.transpose(cache_v, (0, 1, 3, 4, 2)).reshape(depth * n_pool, 512, PAGE_SIZE)
    cache_ik2 = jnp.transpose(cache_idx_k, (0, 1, 3, 2)).reshape(depth * n_pool, IDX_DIM, PAGE_SIZE)
    page_flat = page_table.reshape(-1).astype(I32)
    zeros_c = jnp.zeros((Bp, ML_HEADS, ML_QK_DIM, ML_V_DIM), F32)
    zeros_n = jnp.zeros((Bp, ML_HEADS, ML_QK_DIM), F32)
    zeros_m = jnp.zeros((Bp, ML_HEADS), F32)
    eye_h = jnp.eye(N_HEADS, dtype=BF16)

    outs = {k: [] for k in ("kp", "vp", "ikp", "Cp", "np", "mp", "ks", "vs", "iks", "Cs", "ns", "ms")}
    for l in range(depth):
        W = _layer_weights(l, norm_mix, w_in, q_norm, k_norm, gate_bias, w_branch, w_out, norm_ffn, w_gate_up, w_down)

        def inproj(x, tabs):
            return _inproj(x, tabs, W["g_mix"], W["wmain"], W["wmisc"], W["qn"], W["kn"], W["bias"], W["bd"])

        aq, akf, akb, avf, avb, iq, misc, ikT, mq, mk, mv, mo, gates = inproj(xp, tab_p)
        a_out = _dsa_prompt(aq, iq, misc, ikT, akb, avb)
        gT = jnp.transpose(misc[:, _M_MI:_M_MI + 2 * ML_HEADS])[None]
        m_h, C, n, m = _mlstm(mq, mk, mv, misc, gT, zeros_c, zeros_n, zeros_m, Bp, Lp)
        xp = _merge(xp, a_out, m_h, mo, gates, W["wb0"], W["wb1"], W["wout"])
        xp = _ffn(xp, W["g_ffn"], W["wgu"], W["wd"])
        outs["kp"].append(akf.reshape(Bp, Lp, N_HEADS, HEAD_DIM))
        outs["vp"].append(avf.reshape(Bp, Lp, N_HEADS, HEAD_DIM))
        outs["ikp"].append(misc[:, :IDX_DIM].reshape(Bp, Lp, IDX_DIM))
        outs["Cp"].append(C), outs["np"].append(n), outs["mp"].append(m)

        aq, akf, akb, avf, avb, iq, misc, ikT, mq, mk, mv, mo, gates = inproj(xs, tab_s)
        iqs = iq.reshape(Bd, Ld, N_IDX_HEADS, IDX_DIM).transpose(0, 2, 1, 3).reshape(Bd, N_IDX_HEADS * Ld, IDX_DIM)
        wst = misc[:, _M_IW:_M_IW + N_IDX_HEADS].reshape(Bd, Ld, N_IDX_HEADS).transpose(0, 2, 1)
        wst = jnp.broadcast_to(wst.reshape(Bd, N_IDX_HEADS * Ld, 1), (Bd, N_IDX_HEADS * Ld, LANES))
        pad_rows = lambda t: jnp.pad(t.reshape(Bd, Ld, -1).transpose(0, 2, 1), ((0, 0), (0, 0), (0, LANES - Ld)))
        iknew = pad_rows(misc[:, :IDX_DIM].astype(BF16))
        keys, thr, jsel = _dsa_sample_select(page_flat, iqs, wst, iknew, cache_ik2, l, n_pool, n_pages, Ld)
        qbd = jnp.einsum("bqhd,hg->bhqgd", aq.reshape(Bd, Ld, N_HEADS, HEAD_DIM), eye_h)
        qbd = qbd.reshape(Bd, N_HEADS * Ld, 512)
        a_out = _dsa_sample_attend(page_flat, qbd, keys, thr, jsel, pad_rows(akb), pad_rows(avb),
                                   cache_k2, cache_v2, l, n_pool, n_pages, Ld).reshape(Rs, 512)
        gT = misc[:, _M_MI:_M_MI + 2 * ML_HEADS].reshape(Bd, Ld, 2 * ML_HEADS).transpose(0, 2, 1)
        m_h, C, n, m = _mlstm(mq, mk, mv, misc, gT, state_C[l], state_n[l], state_m[l], Bd, Ld)
        xs = _merge(xs, a_out, m_h, mo, gates, W["wb0"], W["wb1"], W["wout"])
        xs = _ffn(xs, W["g_ffn"], W["wgu"], W["wd"])
        outs["ks"].append(akf.reshape(Bd, Ld, N_HEADS, HEAD_DIM))
        outs["vs"].append(avf.reshape(Bd, Ld, N_HEADS, HEAD_DIM))
        outs["iks"].append(misc[:, :IDX_DIM].reshape(Bd, Ld, IDX_DIM))
        outs["Cs"].append(C), outs["ns"].append(n), outs["ms"].append(m)

    st = lambda k: jnp.stack(outs[k])
    return (xp.reshape(Bp, Lp, D_MODEL), xs.reshape(Bd, Ld, D_MODEL),
            st("kp"), st("vp"), st("ikp"), st("Cp"), st("np"), st("mp"),
            st("ks"), st("vs"), st("iks"), st("Cs"), st("ns"), st("ms"))
```
